```python
import jax, jax.numpy as jnp
from jax import lax
import numpy as np

D_MODEL = 1024
BATCH = 32
SEQ = 2048
DEPTH = 4
DEC_BATCH = 32
DEC_SEQ = 32
PAST_LEN = 1024

CHUNK = 64
GMLP_CHUNK = 128
GMLP_GROUPS = 8
WIDTH_A = D_MODEL
GMLP_HEAD = WIDTH_A // GMLP_GROUPS
WIDTH_B = D_MODEL
CONV_WIDTH = 3
D_FF = ((8 * D_MODEL // 3 + 255) // 256) * 256
EPS = 1e-6
SPLITS = (WIDTH_A, WIDTH_A, WIDTH_B, WIDTH_B, WIDTH_B, D_MODEL, D_MODEL)
IN_COLS = sum(SPLITS)
SPLIT_IDX = tuple(int(i) for i in np.cumsum(SPLITS)[:-1])

kernel_name = "hybrid_gmlp_shortconv_streaming_step"


def rmsnorm(x, g):
    xf = x.astype(jnp.float32)
    r = lax.rsqrt(jnp.mean(xf * xf, axis=-1, keepdims=True) + EPS)
    return (xf * r).astype(x.dtype) * g


def layernorm(x, g, b):
    xf = x.astype(jnp.float32)
    mu = jnp.mean(xf, axis=-1, keepdims=True)
    var = jnp.mean(jnp.square(xf - mu), axis=-1, keepdims=True)
    return ((xf - mu) * lax.rsqrt(var + EPS)).astype(x.dtype) * g + b


def spatial_gate(v, w_s, b_s):
    bsz, L, _ = v.shape
    n_chunks = -(-L // GMLP_CHUNK)
    pad = n_chunks * GMLP_CHUNK - L
    vp = jnp.pad(v, ((0, 0), (0, pad), (0, 0)))
    vr = vp.reshape(bsz, n_chunks, GMLP_CHUNK, GMLP_GROUPS, GMLP_HEAD)
    mask = jnp.tril(jnp.ones((GMLP_CHUNK, GMLP_CHUNK), dtype=bool))
    ws = jnp.where(mask[None], w_s, jnp.zeros_like(w_s))
    out = jnp.einsum('gij,bcjgd->bcigd', ws, vr) + b_s.T[None, None, :, :, None]
    return out.reshape(bsz, n_chunks * GMLP_CHUNK, WIDTH_A)[:, :L]


def mixer(xn, conv_prev, w_in, ln_g, ln_b, w_s, b_s, conv_w, w_pa, w_pb, w_o):
    z = xn @ w_in
    u_a, v_a, bg, cg, hh, ga, gb = jnp.split(z, SPLIT_IDX, axis=-1)
    u = jax.nn.gelu(u_a, approximate=False)
    v = layernorm(jax.nn.gelu(v_a, approximate=False), ln_g, ln_b)
    a = u * spatial_gate(v, w_s, b_s)
    L = xn.shape[1]
    xc = jnp.concatenate([conv_prev, cg * hh], axis=1)
    conv = conv_w[0] * xc[:, :L] + conv_w[1] * xc[:, 1:L + 1] + conv_w[2] * xc[:, 2:]
    bconv = bg * conv
    m = jax.nn.sigmoid(ga) * (a @ w_pa) + jax.nn.sigmoid(gb) * (bconv @ w_pb)
    return m @ w_o, xc[:, -(CONV_WIDTH - 1):], v


def swiglu(x, w_gate, w_up, w_down):
    return (jax.nn.silu(x @ w_gate) * (x @ w_up)) @ w_down


def trunk(x, conv_states, norm_mix, w_in, ln_g, ln_b, w_s, b_s, conv_w,
          w_pa, w_pb, w_o, norm_ffn, w_gate, w_up, w_down, norm_final):
    new_conv, new_v = [], []
    for l in range(DEPTH):
        mo, c_new, v_rows = mixer(rmsnorm(x, norm_mix[l]), conv_states[l], w_in[l], ln_g[l], ln_b[l],
                                  w_s[l], b_s[l], conv_w[l], w_pa[l], w_pb[l], w_o[l])
        x = x + mo
        x = x + swiglu(rmsnorm(x, norm_ffn[l]), w_gate[l], w_up[l], w_down[l])
        new_conv.append(c_new)
        new_v.append(v_rows)
    return rmsnorm(x, norm_final), jnp.stack(new_conv), jnp.stack(new_v)


def setup_inputs(seed: int = 0) -> dict:
    key = jax.random.key(seed)
    ks = jax.random.split(key, 20)
    f32 = jnp.float32
    def nrm(k, shape, scale):
        return jax.random.normal(k, shape, f32) * scale
    return {
        "x_prompt": nrm(ks[0], (BATCH, SEQ, D_MODEL), 1.0),
        "x_sample": nrm(ks[1], (DEC_BATCH, DEC_SEQ, D_MODEL), 1.0),
        "state_conv": nrm(ks[2], (DEPTH, DEC_BATCH, CONV_WIDTH - 1, WIDTH_B), 1.0),
        "norm_mix": 1.0 + nrm(ks[3], (DEPTH, D_MODEL), 0.02),
        "w_in": nrm(ks[4], (DEPTH, D_MODEL, IN_COLS), D_MODEL ** -0.5),
        "gmlp_ln_g": 1.0 + nrm(ks[5], (DEPTH, WIDTH_A), 0.02),
        "gmlp_ln_b": nrm(ks[6], (DEPTH, WIDTH_A), 0.02),
        "w_s": nrm(ks[7], (DEPTH, GMLP_GROUPS, GMLP_CHUNK, GMLP_CHUNK), GMLP_CHUNK ** -0.5),
        "b_s": 1.0 + nrm(ks[8], (DEPTH, GMLP_GROUPS, GMLP_CHUNK), 0.02),
        "conv_w": nrm(ks[9], (DEPTH, CONV_WIDTH, WIDTH_B), CONV_WIDTH ** -0.5),
        "w_pa": nrm(ks[10], (DEPTH, WIDTH_A, D_MODEL), WIDTH_A ** -0.5),
        "w_pb": nrm(ks[11], (DEPTH, WIDTH_B, D_MODEL), WIDTH_B ** -0.5),
        "w_o": nrm(ks[12], (DEPTH, D_MODEL, D_MODEL), 0.5 * D_MODEL ** -0.5),
        "norm_ffn": 1.0 + nrm(ks[13], (DEPTH, D_MODEL), 0.02),
        "w_gate": nrm(ks[14], (DEPTH, D_MODEL, D_FF), D_MODEL ** -0.5),
        "w_up": nrm(ks[15], (DEPTH, D_MODEL, D_FF), D_MODEL ** -0.5),
        "w_down": nrm(ks[16], (DEPTH, D_FF, D_MODEL), 0.5 * D_FF ** -0.5),
        "norm_final": 1.0 + nrm(ks[17], (D_MODEL,), 0.02),
    }


def reference(x_prompt, x_sample, state_conv, norm_mix, w_in, gmlp_ln_g, gmlp_ln_b, w_s, b_s,
              conv_w, w_pa, w_pb, w_o, norm_ffn, w_gate, w_up, w_down, norm_final):
    params = (norm_mix, w_in, gmlp_ln_g, gmlp_ln_b, w_s, b_s, conv_w,
              w_pa, w_pb, w_o, norm_ffn, w_gate, w_up, w_down, norm_final)
    zero_conv = jnp.zeros((DEPTH, x_prompt.shape[0], CONV_WIDTH - 1, WIDTH_B), dtype=x_prompt.dtype)
    y_prompt, new_conv_prompt, _ = trunk(x_prompt, zero_conv, *params)
    y_sample, new_conv_sample, new_gmlp_v_sample = trunk(x_sample, state_conv, *params)
    return (y_prompt, y_sample, new_conv_prompt, new_conv_sample, new_gmlp_v_sample)
```

```python
import functools

import jax
import jax.numpy as jnp
from jax import lax
from jax.experimental import pallas as pl
from jax.experimental.pallas import tpu as pltpu

D_MODEL = 1024
DEPTH = 4
GMLP_CHUNK = 128
GMLP_GROUPS = 8
GMLP_HEAD = D_MODEL // GMLP_GROUPS
CONV_WIDTH = 3
D_FF = 2816
IN_COLS = 7 * D_MODEL
EPS = 1e-6

SUBLANES = 8
HALO_PAD = SUBLANES
PROMPT_TILE = 512
VMEM_LIMIT_BYTES = 56 * 1024 * 1024

_F32 = jnp.float32
_BF16 = jnp.bfloat16


def _gelu(x):
    return (0.5 * x) * (1.0 + lax.erf(x * (0.5 ** 0.5)))


def _rmsnorm_rows(x, gain):
    r = lax.rsqrt(jnp.mean(x * x, axis=-1, keepdims=True) + EPS)
    return (x * r) * gain


def _mixer_tile(x, halo_fn, nm_ref, win_ref, lng_ref, lnb_ref, ws_ref, bsf_ref, cw_ref,
                wpa_ref, wpb_ref, wo_ref, x1_ref, conv_ref, v_ref, xc_ref, a_ref,
                *, nseg, seglen, chunk):
    rows = nseg * seglen
    xn = _rmsnorm_rows(x, nm_ref[...]).astype(_BF16)

    def proj(k):
        return jnp.dot(xn, win_ref[:, k * D_MODEL:(k + 1) * D_MODEL], preferred_element_type=_F32)

    v = _gelu(proj(1))
    mu = jnp.mean(v, axis=-1, keepdims=True)
    vc = v - mu
    var = jnp.mean(vc * vc, axis=-1, keepdims=True)
    v = (vc * lax.rsqrt(var + EPS)) * lng_ref[...] + lnb_ref[...]
    if v_ref is not None:
        v_ref[...] = v.reshape(v_ref.shape)
    vb = v.astype(_BF16)
    u = _gelu(proj(0))

    ri = lax.broadcasted_iota(jnp.int32, (chunk, chunk), 0)
    ci = lax.broadcasted_iota(jnp.int32, (chunk, chunk), 1)
    tril = ci <= ri
    for g in range(GMLP_GROUPS):
        cols = slice(g * GMLP_HEAD, (g + 1) * GMLP_HEAD)
        wsg = jnp.where(tril, ws_ref[g, :chunk, :chunk], jnp.zeros((), _BF16))
        bias = bsf_ref[:chunk, cols]
        for c in range(rows // chunk):
            rws = slice(c * chunk, (c + 1) * chunk)
            sg = jnp.dot(wsg, vb[rws, cols], preferred_element_type=_F32) + bias
            a_ref[rws, cols] = (u[rws, cols] * sg).astype(_BF16)

    halo_fn()
    xc_ref[:, HALO_PAD:HALO_PAD + seglen, :] = (proj(3) * proj(4)).reshape(nseg, seglen, D_MODEL)
    cw = cw_ref[...]
    conv = (cw[0] * xc_ref[:, HALO_PAD - 2:HALO_PAD - 2 + seglen, :]
            + cw[1] * xc_ref[:, HALO_PAD - 1:HALO_PAD - 1 + seglen, :]
            + cw[2] * xc_ref[:, HALO_PAD:HALO_PAD + seglen, :])
    last2 = xc_ref[:, HALO_PAD + seglen - 2:HALO_PAD + seglen, :]
    conv_ref[...] = last2
    xc_ref[:, HALO_PAD - 2:HALO_PAD, :] = last2
    bconv = (proj(2) * conv.reshape(rows, D_MODEL)).astype(_BF16)

    ga = jax.nn.sigmoid(proj(5))
    gb = jax.nn.sigmoid(proj(6))
    m = (ga * jnp.dot(a_ref[...], wpa_ref[...], preferred_element_type=_F32)
         + gb * jnp.dot(bconv, wpb_ref[...], preferred_element_type=_F32))
    x1 = x + jnp.dot(m.astype(_BF16), wo_ref[...], preferred_element_type=_F32)
    x1_ref[...] = x1.reshape(x1_ref.shape)


def _prompt_mixer_kernel(x_ref, nm_ref, win_ref, lng_ref, lnb_ref, ws_ref, bsf_ref, cw_ref,
                         wpa_ref, wpb_ref, wo_ref, x1_ref, conv_ref, xc_ref, a_ref, *, tile):
    def halo():
        @pl.when(pl.program_id(1) == 0)
        def _():
            xc_ref[:, HALO_PAD - 2:HALO_PAD, :] = jnp.zeros((1, 2, D_MODEL), _F32)

    _mixer_tile(x_ref[...], halo, nm_ref, win_ref, lng_ref, lnb_ref, ws_ref, bsf_ref, cw_ref,
                wpa_ref, wpb_ref, wo_ref, x1_ref, conv_ref, None, xc_ref, a_ref,
                nseg=1, seglen=tile, chunk=GMLP_CHUNK)


def _sample_mixer_kernel(x_ref, halo_ref, nm_ref, win_ref, lng_ref, lnb_ref, ws_ref, bsf_ref,
                         cw_ref, wpa_ref, wpb_ref, wo_ref, x1_ref, conv_ref, v_ref, xc_ref, a_ref,
                         *, nseg, seglen):
    def halo():
        xc_ref[:, HALO_PAD - 2:HALO_PAD, :] = halo_ref[...]

    _mixer_tile(x_ref[...].reshape(nseg * seglen, D_MODEL), halo, nm_ref, win_ref, lng_ref,
                lnb_ref, ws_ref, bsf_ref, cw_ref, wpa_ref, wpb_ref, wo_ref, x1_ref, conv_ref,
                v_ref, xc_ref, a_ref, nseg=nseg, seglen=seglen, chunk=seglen)


def _ffn_kernel(x_ref, nf_ref, wg_ref, wu_ref, wd_ref, nfin_ref, y_ref, *, final_norm):
    shape = x_ref.shape
    x = x_ref[...].reshape(-1, D_MODEL)
    xn = _rmsnorm_rows(x, nf_ref[...]).astype(_BF16)
    hg = jnp.dot(xn, wg_ref[...], preferred_element_type=_F32)
    hu = jnp.dot(xn, wu_ref[...], preferred_element_type=_F32)
    hh = (jax.nn.silu(hg) * hu).astype(_BF16)
    y = x + jnp.dot(hh, wd_ref[...], preferred_element_type=_F32)
    if final_norm:
        y = _rmsnorm_rows(y, nfin_ref[...])
    y_ref[...] = y.reshape(shape)


def _resident(shape, layer, grid_rank):
    zeros = (0,) * (len(shape) - 1)
    if grid_rank == 1:
        index_map = lambda i: (layer,) + zeros
    else:
        index_map = lambda b, s: (layer,) + zeros
    return pl.BlockSpec((None,) + tuple(shape[1:]), index_map, pipeline_mode=pl.Buffered(1))


def _mixer_param_specs(params, layer, grid_rank):
    return [_resident(p.shape, layer, grid_rank) for p in params]


def _prompt_mixer(x, params, layer):
    batch, seq, _ = x.shape
    tile = PROMPT_TILE
    grid = (batch, seq // tile)
    x_spec = pl.BlockSpec((None, tile, D_MODEL), lambda b, s: (b, s, 0))
    return pl.pallas_call(
        functools.partial(_prompt_mixer_kernel, tile=tile),
        grid=grid,
        in_specs=[x_spec] + _mixer_param_specs(params, layer, 2),
        out_specs=[x_spec, pl.BlockSpec((1, CONV_WIDTH - 1, D_MODEL), lambda b, s: (b, 0, 0))],
        out_shape=[jax.ShapeDtypeStruct(x.shape, _F32),
                   jax.ShapeDtypeStruct((batch, CONV_WIDTH - 1, D_MODEL), _F32)],
        scratch_shapes=[pltpu.VMEM((1, HALO_PAD + tile, D_MODEL), _F32),
                        pltpu.VMEM((tile, D_MODEL), _BF16)],
        compiler_params=pltpu.CompilerParams(
            dimension_semantics=("arbitrary", "arbitrary"), vmem_limit_bytes=VMEM_LIMIT_BYTES),
        name=f"prompt_mixer_l{layer}",
    )(x, *params)


def _sample_mixer(x, state_conv, params, layer):
    nseq, seglen, _ = x.shape
    nseg = nseq // 2
    grid = (nseq // nseg,)
    x_spec = pl.BlockSpec((nseg, seglen, D_MODEL), lambda i: (i, 0, 0))
    conv_spec = pl.BlockSpec((nseg, CONV_WIDTH - 1, D_MODEL), lambda i: (i, 0, 0))
    halo_spec = pl.BlockSpec((None, nseg, CONV_WIDTH - 1, D_MODEL), lambda i: (layer, i, 0, 0))
    return pl.pallas_call(
        functools.partial(_sample_mixer_kernel, nseg=nseg, seglen=seglen),
        grid=grid,
        in_specs=[x_spec, halo_spec] + _mixer_param_specs(params, layer, 1),
        out_specs=[x_spec, conv_spec, x_spec],
        out_shape=[jax.ShapeDtypeStruct(x.shape, _F32),
                   jax.ShapeDtypeStruct((nseq, CONV_WIDTH - 1, D_MODEL), _F32),
                   jax.ShapeDtypeStruct(x.shape, _F32)],
        scratch_shapes=[pltpu.VMEM((nseg, HALO_PAD + seglen, D_MODEL), _F32),
                        pltpu.VMEM((nseg * seglen, D_MODEL), _BF16)],
        compiler_params=pltpu.CompilerParams(
            dimension_semantics=("arbitrary",), vmem_limit_bytes=VMEM_LIMIT_BYTES),
        name=f"sample_mixer_l{layer}",
    )(x, state_conv, *params)


def _ffn(x, params, norm_final, layer, row_block, name):
    nseq, seqlen, _ = x.shape
    final_norm = layer == DEPTH - 1
    if row_block <= seqlen:
        grid = (nseq, seqlen // row_block)
        x_spec = pl.BlockSpec((None, row_block, D_MODEL), lambda b, s: (b, s, 0))
    else:
        per = row_block // seqlen
        grid = (nseq // per, 1)
        x_spec = pl.BlockSpec((per, seqlen, D_MODEL), lambda b, s: (b, 0, 0))
    nfin_spec = pl.BlockSpec((1, D_MODEL), lambda b, s: (0, 0))
    return pl.pallas_call(
        functools.partial(_ffn_kernel, final_norm=final_norm),
        grid=grid,
        in_specs=[x_spec] + [_resident(p.shape, layer, 2) for p in params] + [nfin_spec],
        out_specs=x_spec,
        out_shape=jax.ShapeDtypeStruct(x.shape, _F32),
        compiler_params=pltpu.CompilerParams(
            dimension_semantics=("arbitrary", "arbitrary"), vmem_limit_bytes=VMEM_LIMIT_BYTES),
        name=f"{name}_ffn_l{layer}",
    )(x, *params, norm_final)


def kernel(x_prompt, x_sample, state_conv, norm_mix, w_in, gmlp_ln_g, gmlp_ln_b, w_s, b_s,
           conv_w, w_pa, w_pb, w_o, norm_ffn, w_gate, w_up, w_down, norm_final):
    row = lambda p: p.reshape(DEPTH, 1, D_MODEL)
    bsf = jnp.repeat(jnp.transpose(b_s, (0, 2, 1)), GMLP_HEAD, axis=2)
    mixer_params = (row(norm_mix), w_in.astype(_BF16), row(gmlp_ln_g), row(gmlp_ln_b),
                    w_s.astype(_BF16), bsf, conv_w, w_pa.astype(_BF16), w_pb.astype(_BF16),
                    w_o.astype(_BF16))
    ffn_params = (row(norm_ffn), w_gate.astype(_BF16), w_up.astype(_BF16), w_down.astype(_BF16))
    nfin = norm_final.reshape(1, D_MODEL)

    xp, xs = x_prompt, x_sample
    conv_p, conv_s, v_s = [], [], []
    for layer in range(DEPTH):
        xp, cp = _prompt_mixer(xp, mixer_params, layer)
        xp = _ffn(xp, ffn_params, nfin, layer, PROMPT_TILE, "prompt")
        xs, cs, vs = _sample_mixer(xs, state_conv, mixer_params, layer)
        xs = _ffn(xs, ffn_params, nfin, layer, xs.shape[0] * xs.shape[1] // 2, "sample")
        conv_p.append(cp)
        conv_s.append(cs)
        v_s.append(vs)
    return (xp, xs, jnp.stack(conv_p), jnp.stack(conv_s), jnp.stack(v_s))
```

```python
import functools

import jax
import jax.numpy as jnp
from jax import lax
from jax.experimental import pallas as pl
from jax.experimental.pallas import tpu as pltpu

D_MODEL = 1024
DEPTH = 4
GMLP_CHUNK = 128
GMLP_GROUPS = 8
GMLP_HEAD = D_MODEL // GMLP_GROUPS
CONV_WIDTH = 3
EPS = 1e-6

SUBLANES = 8
HALO_PAD = SUBLANES
TILE_ROWS = 1024
SUB_ROWS = 512
VMEM_LIMIT_BYTES = 56 * 1024 * 1024

_F32 = jnp.float32
_BF16 = jnp.bfloat16


def _gelu(x):
    return (0.5 * x) * (1.0 + lax.erf(x * (0.5 ** 0.5)))


def _rmsnorm_rows(x, gain):
    r = lax.rsqrt(jnp.mean(x * x, axis=-1, keepdims=True) + EPS)
    return (x * r) * gain


def _mixer_rows(x, w, scratch, set_halo, store_v, *, nseg, seglen, chunk):
    nm_ref, win_ref, lng_ref, lnb_ref, ws_ref, bsf_ref, cw_ref, wpa_ref, wpb_ref, wo_ref = w
    xc_ref, a_ref = scratch
    rows = nseg * seglen
    xn = _rmsnorm_rows(x, nm_ref[...]).astype(_BF16)

    def proj(k):
        return jnp.dot(xn, win_ref[:, k * D_MODEL:(k + 1) * D_MODEL], preferred_element_type=_F32)

    v = _gelu(proj(1))
    mu = jnp.mean(v, axis=-1, keepdims=True)
    vc = v - mu
    var = jnp.mean(vc * vc, axis=-1, keepdims=True)
    v = (vc * lax.rsqrt(var + EPS)) * lng_ref[...] + lnb_ref[...]
    if store_v is not None:
        store_v(v)
    vb = v.astype(_BF16)
    u = _gelu(proj(0))

    ri = lax.broadcasted_iota(jnp.int32, (chunk, chunk), 0)
    ci = lax.broadcasted_iota(jnp.int32, (chunk, chunk), 1)
    tril = ci <= ri
    for g in range(GMLP_GROUPS):
        cols = slice(g * GMLP_HEAD, (g + 1) * GMLP_HEAD)
        wsg = jnp.where(tril, ws_ref[g, :chunk, :chunk], jnp.zeros((), _BF16))
        bias = bsf_ref[:chunk, cols]
        for c in range(rows // chunk):
            rws = slice(c * chunk, (c + 1) * chunk)
            sg = jnp.dot(wsg, vb[rws, cols], preferred_element_type=_F32) + bias
            a_ref[rws, cols] = (u[rws, cols] * sg).astype(_BF16)

    set_halo()
    xc_ref[:, HALO_PAD:HALO_PAD + seglen, :] = (proj(3) * proj(4)).reshape(nseg, seglen, D_MODEL)
    cw = cw_ref[...]
    conv = (cw[0] * xc_ref[:, HALO_PAD - 2:HALO_PAD - 2 + seglen, :]
            + cw[1] * xc_ref[:, HALO_PAD - 1:HALO_PAD - 1 + seglen, :]
            + cw[2] * xc_ref[:, HALO_PAD:HALO_PAD + seglen, :])
    last2 = xc_ref[:, HALO_PAD + seglen - 2:HALO_PAD + seglen, :]
    xc_ref[:, HALO_PAD - 2:HALO_PAD, :] = last2
    bconv = (proj(2) * conv.reshape(rows, D_MODEL)).astype(_BF16)

    ga = jax.nn.sigmoid(proj(5))
    gb = jax.nn.sigmoid(proj(6))
    m = (ga * jnp.dot(a_ref[...], wpa_ref[...], preferred_element_type=_F32)
         + gb * jnp.dot(bconv, wpb_ref[...], preferred_element_type=_F32))
    return x + jnp.dot(m.astype(_BF16), wo_ref[...], preferred_element_type=_F32), last2


def _prompt_mixer_kernel(x_ref, *refs, tile, sub):
    w, (x1_ref, conv_ref), scratch = refs[:10], refs[10:12], refs[12:]
    xc_ref = scratch[0]
    for s in range(tile // sub):
        rws = slice(s * sub, (s + 1) * sub)

        def set_halo(first=(s == 0)):
            if first:
                @pl.when(pl.program_id(1) == 0)
                def _():
                    xc_ref[:, HALO_PAD - 2:HALO_PAD, :] = jnp.zeros((1, 2, D_MODEL), _F32)

        x1, last2 = _mixer_rows(x_ref[rws, :], w, scratch, set_halo, None,
                                nseg=1, seglen=sub, chunk=GMLP_CHUNK)
        x1_ref[rws, :] = x1
        if s == tile // sub - 1:
            conv_ref[...] = last2


def _sample_mixer_kernel(x_ref, halo_ref, *refs, nseq, seglen, sub):
    w, (x1_ref, conv_ref, v_ref), scratch = refs[:10], refs[10:13], refs[13:]
    xc_ref = scratch[0]
    nseg = sub // seglen
    for s in range(nseq // nseg):
        sgs = slice(s * nseg, (s + 1) * nseg)

        def set_halo(sgs=sgs):
            xc_ref[:, HALO_PAD - 2:HALO_PAD, :] = halo_ref[sgs]

        def store_v(v, sgs=sgs):
            v_ref[sgs] = v.reshape(nseg, seglen, D_MODEL)

        x1, last2 = _mixer_rows(x_ref[sgs].reshape(sub, D_MODEL), w, scratch, set_halo,
                                store_v, nseg=nseg, seglen=seglen, chunk=seglen)
        x1_ref[sgs] = x1.reshape(nseg, seglen, D_MODEL)
        conv_ref[sgs] = last2


def _ffn_kernel(x_ref, nf_ref, wg_ref, wu_ref, wd_ref, nfin_ref, y_ref, *, sub, final_norm):
    lead = sub if x_ref.ndim == 2 else sub // x_ref.shape[1]
    for s in range(x_ref.shape[0] // lead):
        sl = slice(s * lead, (s + 1) * lead)
        x = x_ref[sl].reshape(sub, D_MODEL)
        xn = _rmsnorm_rows(x, nf_ref[...]).astype(_BF16)
        hg = jnp.dot(xn, wg_ref[...], preferred_element_type=_F32)
        hu = jnp.dot(xn, wu_ref[...], preferred_element_type=_F32)
        hh = (jax.nn.silu(hg) * hu).astype(_BF16)
        y = x + jnp.dot(hh, wd_ref[...], preferred_element_type=_F32)
        if final_norm:
            y = _rmsnorm_rows(y, nfin_ref[...])
        y_ref[sl] = y.reshape((lead,) + tuple(y_ref.shape[1:]))


def _resident(shape, layer, grid_rank):
    zeros = (0,) * (len(shape) - 1)
    if grid_rank == 1:
        index_map = lambda i: (layer,) + zeros
    else:
        index_map = lambda b, s: (layer,) + zeros
    return pl.BlockSpec((None,) + tuple(shape[1:]), index_map, pipeline_mode=pl.Buffered(1))


def _mixer_scratch(nseg, seglen):
    return [pltpu.VMEM((nseg, HALO_PAD + seglen, D_MODEL), _F32),
            pltpu.VMEM((nseg * seglen, D_MODEL), _BF16)]


def _prompt_mixer(x, params, layer):
    batch, seq, _ = x.shape
    x_spec = pl.BlockSpec((None, TILE_ROWS, D_MODEL), lambda b, s: (b, s, 0))
    return pl.pallas_call(
        functools.partial(_prompt_mixer_kernel, tile=TILE_ROWS, sub=SUB_ROWS),
        grid=(batch, seq // TILE_ROWS),
        in_specs=[x_spec] + [_resident(p.shape, layer, 2) for p in params],
        out_specs=[x_spec, pl.BlockSpec((1, CONV_WIDTH - 1, D_MODEL), lambda b, s: (b, 0, 0))],
        out_shape=[jax.ShapeDtypeStruct(x.shape, _F32),
                   jax.ShapeDtypeStruct((batch, CONV_WIDTH - 1, D_MODEL), _F32)],
        scratch_shapes=_mixer_scratch(1, SUB_ROWS),
        compiler_params=pltpu.CompilerParams(
            dimension_semantics=("arbitrary", "arbitrary"), vmem_limit_bytes=VMEM_LIMIT_BYTES),
        name=f"prompt_mixer_l{layer}",
    )(x, *params)


def _sample_mixer(x, state_conv, params, layer):
    nseq, seglen, _ = x.shape
    nseg = SUB_ROWS // seglen
    whole = lambda i: (0, 0, 0)
    x_spec = pl.BlockSpec((nseq, seglen, D_MODEL), whole)
    conv_spec = pl.BlockSpec((nseq, CONV_WIDTH - 1, D_MODEL), whole)
    halo_spec = pl.BlockSpec((None, nseq, CONV_WIDTH - 1, D_MODEL), lambda i: (layer, 0, 0, 0))
    return pl.pallas_call(
        functools.partial(_sample_mixer_kernel, nseq=nseq, seglen=seglen, sub=SUB_ROWS),
        grid=(1,),
        in_specs=[x_spec, halo_spec] + [_resident(p.shape, layer, 1) for p in params],
        out_specs=[x_spec, conv_spec, x_spec],
        out_shape=[jax.ShapeDtypeStruct(x.shape, _F32),
                   jax.ShapeDtypeStruct((nseq, CONV_WIDTH - 1, D_MODEL), _F32),
                   jax.ShapeDtypeStruct(x.shape, _F32)],
        scratch_shapes=_mixer_scratch(nseg, seglen),
        compiler_params=pltpu.CompilerParams(
            dimension_semantics=("arbitrary",), vmem_limit_bytes=VMEM_LIMIT_BYTES),
        name=f"sample_mixer_l{layer}",
    )(x, state_conv, *params)


def _ffn(x, params, norm_final, layer, name):
    nseq, seqlen, _ = x.shape
    if TILE_ROWS <= seqlen:
        grid = (nseq, seqlen // TILE_ROWS)
        x_spec = pl.BlockSpec((None, TILE_ROWS, D_MODEL), lambda b, s: (b, s, 0))
    else:
        per = TILE_ROWS // seqlen
        grid = (nseq // per, 1)
        x_spec = pl.BlockSpec((per, seqlen, D_MODEL), lambda b, s: (b, 0, 0))
    nfin_spec = pl.BlockSpec((1, D_MODEL), lambda b, s: (0, 0))
    return pl.pallas_call(
        functools.partial(_ffn_kernel, sub=SUB_ROWS, final_norm=layer == DEPTH - 1),
        grid=grid,
        in_specs=[x_spec] + [_resident(p.shape, layer, 2) for p in params] + [nfin_spec],
        out_specs=x_spec,
        out_shape=jax.ShapeDtypeStruct(x.shape, _F32),
        compiler_params=pltpu.CompilerParams(
            dimension_semantics=("arbitrary", "arbitrary"), vmem_limit_bytes=VMEM_LIMIT_BYTES),
        name=f"{name}_ffn_l{layer}",
    )(x, *params, norm_final)


def kernel(x_prompt, x_sample, state_conv, norm_mix, w_in, gmlp_ln_g, gmlp_ln_b, w_s, b_s,
           conv_w, w_pa, w_pb, w_o, norm_ffn, w_gate, w_up, w_down, norm_final):
    assert x_prompt.shape[1] % TILE_ROWS == 0 and TILE_ROWS % SUB_ROWS == 0
    assert SUB_ROWS % GMLP_CHUNK == 0 and SUB_ROWS % x_sample.shape[1] == 0
    assert (x_sample.shape[0] * x_sample.shape[1]) % SUB_ROWS == 0
    row = lambda p: p.reshape(DEPTH, 1, D_MODEL)
    bsf = jnp.repeat(jnp.transpose(b_s, (0, 2, 1)), GMLP_HEAD, axis=2)
    mixer_params = (row(norm_mix), w_in.astype(_BF16), row(gmlp_ln_g), row(gmlp_ln_b),
                    w_s.astype(_BF16), bsf, conv_w, w_pa.astype(_BF16), w_pb.astype(_BF16),
                    w_o.astype(_BF16))
    ffn_params = (row(norm_ffn), w_gate.astype(_BF16), w_up.astype(_BF16), w_down.astype(_BF16))
    nfin = norm_final.reshape(1, D_MODEL)

    xp, xs = x_prompt, x_sample
    conv_p, conv_s, v_s = [], [], []
    for layer in range(DEPTH):
        xp, cp = _prompt_mixer(xp, mixer_params, layer)
        xp = _ffn(xp, ffn_params, nfin, layer, "prompt")
        xs, cs, vs = _sample_mixer(xs, state_conv, mixer_params, layer)
        xs = _ffn(xs, ffn_params, nfin, layer, "sample")
        conv_p.append(cp)
        conv_s.append(cs)
        v_s.append(vs)
    return (xp, xs, jnp.stack(conv_p), jnp.stack(conv_s), jnp.stack(v_s))
```

```python
import functools

import jax
import jax.numpy as jnp
from jax import lax
from jax.experimental import pallas as pl
from jax.experimental.pallas import tpu as pltpu

D_MODEL = 1024
DEPTH = 4
GMLP_CHUNK = 128
GMLP_GROUPS = 8
GMLP_HEAD = D_MODEL // GMLP_GROUPS
CONV_WIDTH = 3
EPS = 1e-6

SUBLANES = 8
HALO_PAD = SUBLANES
TILE_ROWS = 1024
SUB_ROWS = 256
FFN_SUB_ROWS = 1024
VMEM_LIMIT_BYTES = 56 * 1024 * 1024

_F32 = jnp.float32
_BF16 = jnp.bfloat16


def _gelu(x):
    return (0.5 * x) * (1.0 + lax.erf(x * (0.5 ** 0.5)))


def _rmsnorm_rows(x, gain):
    r = lax.rsqrt(jnp.mean(x * x, axis=-1, keepdims=True) + EPS)
    return (x * r) * gain


def _mixer_rows(x, w, scratch, set_halo, store_v, *, nseg, seglen, chunk):
    nm_ref, win_ref, lng_ref, lnb_ref, ws_ref, bsf_ref, cw_ref, wpa_ref, wpb_ref, wo_ref = w
    xc_ref, a_ref = scratch
    rows = nseg * seglen
    xn = _rmsnorm_rows(x, nm_ref[...]).astype(_BF16)

    def proj(k):
        return jnp.dot(xn, win_ref[:, k * D_MODEL:(k + 1) * D_MODEL], preferred_element_type=_F32)

    v = _gelu(proj(1))
    mu = jnp.mean(v, axis=-1, keepdims=True)
    vc = v - mu
    var = jnp.mean(vc * vc, axis=-1, keepdims=True)
    v = (vc * lax.rsqrt(var + EPS)) * lng_ref[...] + lnb_ref[...]
    if store_v is not None:
        store_v(v)
    vb = v.astype(_BF16)
    u = _gelu(proj(0))

    ri = lax.broadcasted_iota(jnp.int32, (chunk, chunk), 0)
    ci = lax.broadcasted_iota(jnp.int32, (chunk, chunk), 1)
    tril = ci <= ri
    for g in range(GMLP_GROUPS):
        cols = slice(g * GMLP_HEAD, (g + 1) * GMLP_HEAD)
        wsg = jnp.where(tril, ws_ref[g, :chunk, :chunk], jnp.zeros((), _BF16))
        bias = bsf_ref[:chunk, cols]
        for c in range(rows // chunk):
            rws = slice(c * chunk, (c + 1) * chunk)
            sg = jnp.dot(wsg, vb[rws, cols], preferred_element_type=_F32) + bias
            a_ref[rws, cols] = (u[rws, cols] * sg).astype(_BF16)

    set_halo()
    xc_ref[:, HALO_PAD:HALO_PAD + seglen, :] = (proj(3) * proj(4)).reshape(nseg, seglen, D_MODEL)
    cw = cw_ref[...]
    conv = (cw[0] * xc_ref[:, HALO_PAD - 2:HALO_PAD - 2 + seglen, :]
            + cw[1] * xc_ref[:, HALO_PAD - 1:HALO_PAD - 1 + seglen, :]
            + cw[2] * xc_ref[:, HALO_PAD:HALO_PAD + seglen, :])
    last2 = xc_ref[:, HALO_PAD + seglen - 2:HALO_PAD + seglen, :]
    xc_ref[:, HALO_PAD - 2:HALO_PAD, :] = last2
    bconv = (proj(2) * conv.reshape(rows, D_MODEL)).astype(_BF16)

    ga = jax.nn.sigmoid(proj(5))
    gb = jax.nn.sigmoid(proj(6))
    m = (ga * jnp.dot(a_ref[...], wpa_ref[...], preferred_element_type=_F32)
         + gb * jnp.dot(bconv, wpb_ref[...], preferred_element_type=_F32))
    return x + jnp.dot(m.astype(_BF16), wo_ref[...], preferred_element_type=_F32), last2


def _prompt_mixer_kernel(x_ref, *refs, tile, sub):
    w, (x1_ref, conv_ref), scratch = refs[:10], refs[10:12], refs[12:]
    xc_ref = scratch[0]
    for s in range(tile // sub):
        rws = slice(s * sub, (s + 1) * sub)

        def set_halo(first=(s == 0)):
            if first:
                @pl.when(pl.program_id(1) == 0)
                def _():
                    xc_ref[:, HALO_PAD - 2:HALO_PAD, :] = jnp.zeros((1, 2, D_MODEL), _F32)

        x1, last2 = _mixer_rows(x_ref[rws, :], w, scratch, set_halo, None,
                                nseg=1, seglen=sub, chunk=GMLP_CHUNK)
        x1_ref[rws, :] = x1
        if s == tile // sub - 1:
            conv_ref[...] = last2


def _sample_mixer_kernel(x_ref, halo_ref, *refs, nseq, seglen, sub):
    w, (x1_ref, conv_ref, v_ref), scratch = refs[:10], refs[10:13], refs[13:]
    xc_ref = scratch[0]
    nseg = sub // seglen
    for s in range(nseq // nseg):
        sgs = slice(s * nseg, (s + 1) * nseg)

        def set_halo(sgs=sgs):
            xc_ref[:, HALO_PAD - 2:HALO_PAD, :] = halo_ref[sgs]

        def store_v(v, sgs=sgs):
            v_ref[sgs] = v.reshape(nseg, seglen, D_MODEL)

        x1, last2 = _mixer_rows(x_ref[sgs].reshape(sub, D_MODEL), w, scratch, set_halo,
                                store_v, nseg=nseg, seglen=seglen, chunk=seglen)
        x1_ref[sgs] = x1.reshape(nseg, seglen, D_MODEL)
        conv_ref[sgs] = last2


def _ffn_kernel(x_ref, nf_ref, wg_ref, wu_ref, wd_ref, nfin_ref, y_ref, *, sub, final_norm):
    lead = sub if x_ref.ndim == 2 else sub // x_ref.shape[1]
    for s in range(x_ref.shape[0] // lead):
        sl = slice(s * lead, (s + 1) * lead)
        x = x_ref[sl].reshape(sub, D_MODEL)
        xn = _rmsnorm_rows(x, nf_ref[...]).astype(_BF16)
        hg = jnp.dot(xn, wg_ref[...], preferred_element_type=_F32)
        hu = jnp.dot(xn, wu_ref[...], preferred_element_type=_F32)
        hh = (jax.nn.silu(hg) * hu).astype(_BF16)
        y = x + jnp.dot(hh, wd_ref[...], preferred_element_type=_F32)
        if final_norm:
            y = _rmsnorm_rows(y, nfin_ref[...])
        y_ref[sl] = y.reshape((lead,) + tuple(y_ref.shape[1:]))


def _resident(shape, layer, grid_rank):
    zeros = (0,) * (len(shape) - 1)
    if grid_rank == 1:
        index_map = lambda i: (layer,) + zeros
    else:
        index_map = lambda b, s: (layer,) + zeros
    return pl.BlockSpec((None,) + tuple(shape[1:]), index_map, pipeline_mode=pl.Buffered(1))


def _mixer_scratch(nseg, seglen):
    return [pltpu.VMEM((nseg, HALO_PAD + seglen, D_MODEL), _F32),
            pltpu.VMEM((nseg * seglen, D_MODEL), _BF16)]


def _prompt_mixer(x, params, layer):
    batch, seq, _ = x.shape
    x_spec = pl.BlockSpec((None, TILE_ROWS, D_MODEL), lambda b, s: (b, s, 0))
    return pl.pallas_call(
        functools.partial(_prompt_mixer_kernel, tile=TILE_ROWS, sub=SUB_ROWS),
        grid=(batch, seq // TILE_ROWS),
        in_specs=[x_spec] + [_resident(p.shape, layer, 2) for p in params],
        out_specs=[x_spec, pl.BlockSpec((1, CONV_WIDTH - 1, D_MODEL), lambda b, s: (b, 0, 0))],
        out_shape=[jax.ShapeDtypeStruct(x.shape, _F32),
                   jax.ShapeDtypeStruct((batch, CONV_WIDTH - 1, D_MODEL), _F32)],
        scratch_shapes=_mixer_scratch(1, SUB_ROWS),
        compiler_params=pltpu.CompilerParams(
            dimension_semantics=("arbitrary", "arbitrary"), vmem_limit_bytes=VMEM_LIMIT_BYTES),
        name=f"prompt_mixer_l{layer}",
    )(x, *params)


def _sample_mixer(x, state_conv, params, layer):
    nseq, seglen, _ = x.shape
    nseg = SUB_ROWS // seglen
    whole = lambda i: (0, 0, 0)
    x_spec = pl.BlockSpec((nseq, seglen, D_MODEL), whole)
    conv_spec = pl.BlockSpec((nseq, CONV_WIDTH - 1, D_MODEL), whole)
    halo_spec = pl.BlockSpec((None, nseq, CONV_WIDTH - 1, D_MODEL), lambda i: (layer, 0, 0, 0))
    return pl.pallas_call(
        functools.partial(_sample_mixer_kernel, nseq=nseq, seglen=seglen, sub=SUB_ROWS),
        grid=(1,),
        in_specs=[x_spec, halo_spec] + [_resident(p.shape, layer, 1) for p in params],
        out_specs=[x_spec, conv_spec, x_spec],
        out_shape=[jax.ShapeDtypeStruct(x.shape, _F32),
                   jax.ShapeDtypeStruct((nseq, CONV_WIDTH - 1, D_MODEL), _F32),
                   jax.ShapeDtypeStruct(x.shape, _F32)],
        scratch_shapes=_mixer_scratch(nseg, seglen),
        compiler_params=pltpu.CompilerParams(
            dimension_semantics=("arbitrary",), vmem_limit_bytes=VMEM_LIMIT_BYTES),
        name=f"sample_mixer_l{layer}",
    )(x, state_conv, *params)


def _ffn(x, params, norm_final, layer, name):
    nseq, seqlen, _ = x.shape
    if TILE_ROWS <= seqlen:
        grid = (nseq, seqlen // TILE_ROWS)
        x_spec = pl.BlockSpec((None, TILE_ROWS, D_MODEL), lambda b, s: (b, s, 0))
    else:
        per = TILE_ROWS // seqlen
        grid = (nseq // per, 1)
        x_spec = pl.BlockSpec((per, seqlen, D_MODEL), lambda b, s: (b, 0, 0))
    nfin_spec = pl.BlockSpec((1, D_MODEL), lambda b, s: (0, 0))
    return pl.pallas_call(
        functools.partial(_ffn_kernel, sub=FFN_SUB_ROWS, final_norm=layer == DEPTH - 1),
        grid=grid,
        in_specs=[x_spec] + [_resident(p.shape, layer, 2) for p in params] + [nfin_spec],
        out_specs=x_spec,
        out_shape=jax.ShapeDtypeStruct(x.shape, _F32),
        compiler_params=pltpu.CompilerParams(
            dimension_semantics=("arbitrary", "arbitrary"), vmem_limit_bytes=VMEM_LIMIT_BYTES),
        name=f"{name}_ffn_l{layer}",
    )(x, *params, norm_final)


def kernel(x_prompt, x_sample, state_conv, norm_mix, w_in, gmlp_ln_g, gmlp_ln_b, w_s, b_s,
           conv_w, w_pa, w_pb, w_o, norm_ffn, w_gate, w_up, w_down, norm_final):
    assert x_prompt.shape[1] % TILE_ROWS == 0 and TILE_ROWS % SUB_ROWS == 0
    assert SUB_ROWS % GMLP_CHUNK == 0 and SUB_ROWS % x_sample.shape[1] == 0
    assert (x_sample.shape[0] * x_sample.shape[1]) % SUB_ROWS == 0
    assert TILE_ROWS % FFN_SUB_ROWS == 0 and FFN_SUB_ROWS % x_sample.shape[1] == 0
    row = lambda p: p.reshape(DEPTH, 1, D_MODEL)
    bsf = jnp.repeat(jnp.transpose(b_s, (0, 2, 1)), GMLP_HEAD, axis=2)
    mixer_params = (row(norm_mix), w_in.astype(_BF16), row(gmlp_ln_g), row(gmlp_ln_b),
                    w_s.astype(_BF16), bsf, conv_w, w_pa.astype(_BF16), w_pb.astype(_BF16),
                    w_o.astype(_BF16))
    ffn_params = (row(norm_ffn), w_gate.astype(_BF16), w_up.astype(_BF16), w_down.astype(_BF16))
    nfin = norm_final.reshape(1, D_MODEL)

    xp, xs = x_prompt, x_sample
    conv_p, conv_s, v_s = [], [], []
    for layer in range(DEPTH):
        xp, cp = _prompt_mixer(xp, mixer_params, layer)
        xp = _ffn(xp, ffn_params, nfin, layer, "prompt")
        xs, cs, vs = _sample_mixer(xs, state_conv, mixer_params, layer)
        xs = _ffn(xs, ffn_params, nfin, layer, "sample")
        conv_p.append(cp)
        conv_s.append(cs)
        v_s.append(vs)
    return (xp, xs, jnp.stack(conv_p), jnp.stack(conv_s), jnp.stack(v_s))
```

```python
import functools

import jax
import jax.numpy as jnp
from jax import lax
from jax.experimental import pallas as pl
from jax.experimental.pallas import tpu as pltpu

D_MODEL = 1024
DEPTH = 4
GMLP_CHUNK = 128
GMLP_GROUPS = 8
GMLP_HEAD = D_MODEL // GMLP_GROUPS
CONV_WIDTH = 3
D_FF = 2816
EPS = 1e-6

SUBLANES = 8
LANES = 128
HALO_PAD = SUBLANES
TILE_ROWS = 1024
SUB_ROWS = 256
FFN_SUB_ROWS = 1024
VMEM_LIMIT_BYTES = 56 * 1024 * 1024

_F32 = jnp.float32
_BF16 = jnp.bfloat16


def _gelu(x):
    return (0.5 * x) * (1.0 + lax.erf(x * (0.5 ** 0.5)))


def _rmsnorm_rows(x, gain):
    r = lax.rsqrt(jnp.mean(x * x, axis=-1, keepdims=True) + EPS)
    return (x * r) * gain


def _mixer_rows(x, w, scratch, set_halo, store_v, *, nseg, seglen, chunk):
    nm_ref, win_ref, lng_ref, lnb_ref, ws_ref, bsf_ref, cw_ref, wpa_ref, wpb_ref, wo_ref = w
    xc_ref, a_ref = scratch
    rows = nseg * seglen
    xn = _rmsnorm_rows(x, nm_ref[...]).astype(_BF16)

    def proj(k):
        return jnp.dot(xn, win_ref[:, k * D_MODEL:(k + 1) * D_MODEL], preferred_element_type=_F32)

    v = _gelu(proj(1))
    mu = jnp.mean(v, axis=-1, keepdims=True)
    vc = v - mu
    var = jnp.mean(vc * vc, axis=-1, keepdims=True)
    v = (vc * lax.rsqrt(var + EPS)) * lng_ref[...] + lnb_ref[...]
    if store_v is not None:
        store_v(v)
    vb = v.astype(_BF16)
    u = _gelu(proj(0))

    ri = lax.broadcasted_iota(jnp.int32, (chunk, chunk), 0)
    ci = lax.broadcasted_iota(jnp.int32, (chunk, chunk), 1)
    tril = ci <= ri
    for g in range(GMLP_GROUPS):
        cols = slice(g * GMLP_HEAD, (g + 1) * GMLP_HEAD)
        wsg = jnp.where(tril, ws_ref[g, :chunk, :chunk], jnp.zeros((), _BF16))
        bias = bsf_ref[:chunk, cols]
        for c in range(rows // chunk):
            rws = slice(c * chunk, (c + 1) * chunk)
            sg = jnp.dot(wsg, vb[rws, cols], preferred_element_type=_F32) + bias
            a_ref[rws, cols] = (u[rws, cols] * sg).astype(_BF16)

    set_halo()
    xc_ref[:, HALO_PAD:HALO_PAD + seglen, :] = (proj(3) * proj(4)).reshape(nseg, seglen, D_MODEL)
    cw = cw_ref[...]
    conv = (cw[0] * xc_ref[:, HALO_PAD - 2:HALO_PAD - 2 + seglen, :]
            + cw[1] * xc_ref[:, HALO_PAD - 1:HALO_PAD - 1 + seglen, :]
            + cw[2] * xc_ref[:, HALO_PAD:HALO_PAD + seglen, :])
    last2 = xc_ref[:, HALO_PAD + seglen - 2:HALO_PAD + seglen, :]
    xc_ref[:, HALO_PAD - 2:HALO_PAD, :] = last2
    bconv = (proj(2) * conv.reshape(rows, D_MODEL)).astype(_BF16)

    ga = jax.nn.sigmoid(proj(5))
    gb = jax.nn.sigmoid(proj(6))
    m = (ga * jnp.dot(a_ref[...], wpa_ref[:, :D_MODEL], preferred_element_type=_F32)
         + gb * jnp.dot(bconv, wpb_ref[:, :D_MODEL], preferred_element_type=_F32))
    return x + jnp.dot(m.astype(_BF16), wo_ref[:, :D_MODEL], preferred_element_type=_F32), last2


def _prompt_mixer_kernel(x_ref, *refs, tile, sub):
    w, (x1_ref, conv_ref), scratch = refs[:10], refs[10:12], refs[12:]
    xc_ref = scratch[0]
    for s in range(tile // sub):
        rws = slice(s * sub, (s + 1) * sub)

        def set_halo(first=(s == 0)):
            if first:
                @pl.when(pl.program_id(1) == 0)
                def _():
                    xc_ref[:, HALO_PAD - 2:HALO_PAD, :] = jnp.zeros((1, 2, D_MODEL), _F32)

        x1, last2 = _mixer_rows(x_ref[rws, :], w, scratch, set_halo, None,
                                nseg=1, seglen=sub, chunk=GMLP_CHUNK)
        x1_ref[rws, :] = x1
        if s == tile // sub - 1:
            conv_ref[...] = last2


def _sample_mixer_kernel(x_ref, halo_ref, *refs, nseq, seglen, sub):
    w, (x1_ref, conv_ref, v_ref), scratch = refs[:10], refs[10:13], refs[13:]
    xc_ref = scratch[0]
    nseg = sub // seglen
    for s in range(nseq // nseg):
        sgs = slice(s * nseg, (s + 1) * nseg)

        def set_halo(sgs=sgs):
            xc_ref[:, HALO_PAD - 2:HALO_PAD, :] = halo_ref[sgs]

        def store_v(v, sgs=sgs):
            v_ref[sgs] = v.reshape(nseg, seglen, D_MODEL)

        x1, last2 = _mixer_rows(x_ref[sgs].reshape(sub, D_MODEL), w, scratch, set_halo,
                                store_v, nseg=nseg, seglen=seglen, chunk=seglen)
        x1_ref[sgs] = x1.reshape(nseg, seglen, D_MODEL)
        conv_ref[sgs] = last2


def _ffn_kernel(x_ref, nf_ref, wg_ref, wu_ref, wd_ref, nfin_ref, y_ref, *, sub, final_norm):
    lead = sub if x_ref.ndim == 2 else sub // x_ref.shape[1]
    for s in range(x_ref.shape[0] // lead):
        sl = slice(s * lead, (s + 1) * lead)
        x = x_ref[sl].reshape(sub, D_MODEL)
        xn = _rmsnorm_rows(x, nf_ref[...]).astype(_BF16)
        hg = jnp.dot(xn, wg_ref[:, :D_FF], preferred_element_type=_F32)
        hu = jnp.dot(xn, wu_ref[:, :D_FF], preferred_element_type=_F32)
        hh = (jax.nn.silu(hg) * hu).astype(_BF16)
        y = x + jnp.dot(hh, wd_ref[:, :D_MODEL], preferred_element_type=_F32)
        if final_norm:
            y = _rmsnorm_rows(y, nfin_ref[...])
        y_ref[sl] = y.reshape((lead,) + tuple(y_ref.shape[1:]))


def _resident(shape, layer, grid_rank):
    zeros = (0,) * (len(shape) - 1)
    if grid_rank == 1:
        index_map = lambda i: (layer,) + zeros
    else:
        index_map = lambda b, s: (layer,) + zeros
    return pl.BlockSpec((None,) + tuple(shape[1:]), index_map, pipeline_mode=pl.Buffered(1))


def _mixer_scratch(nseg, seglen):
    return [pltpu.VMEM((nseg, HALO_PAD + seglen, D_MODEL), _F32),
            pltpu.VMEM((nseg * seglen, D_MODEL), _BF16)]


def _prompt_mixer(x, params, layer):
    batch, seq, _ = x.shape
    x_spec = pl.BlockSpec((None, TILE_ROWS, D_MODEL), lambda b, s: (b, s, 0))
    return pl.pallas_call(
        functools.partial(_prompt_mixer_kernel, tile=TILE_ROWS, sub=SUB_ROWS),
        grid=(batch, seq // TILE_ROWS),
        in_specs=[x_spec] + [_resident(p.shape, layer, 2) for p in params],
        out_specs=[x_spec, pl.BlockSpec((1, CONV_WIDTH - 1, D_MODEL), lambda b, s: (b, 0, 0))],
        out_shape=[jax.ShapeDtypeStruct(x.shape, _F32),
                   jax.ShapeDtypeStruct((batch, CONV_WIDTH - 1, D_MODEL), _F32)],
        scratch_shapes=_mixer_scratch(1, SUB_ROWS),
        compiler_params=pltpu.CompilerParams(
            dimension_semantics=("arbitrary", "arbitrary"), vmem_limit_bytes=VMEM_LIMIT_BYTES),
        name=f"prompt_mixer_l{layer}",
    )(x, *params)


def _sample_mixer(x, state_conv, params, layer):
    nseq, seglen, _ = x.shape
    nseg = SUB_ROWS // seglen
    whole = lambda i: (0, 0, 0)
    x_spec = pl.BlockSpec((nseq, seglen, D_MODEL), whole)
    conv_spec = pl.BlockSpec((nseq, CONV_WIDTH - 1, D_MODEL), whole)
    halo_spec = pl.BlockSpec((None, nseq, CONV_WIDTH - 1, D_MODEL), lambda i: (layer, 0, 0, 0))
    return pl.pallas_call(
        functools.partial(_sample_mixer_kernel, nseq=nseq, seglen=seglen, sub=SUB_ROWS),
        grid=(1,),
        in_specs=[x_spec, halo_spec] + [_resident(p.shape, layer, 1) for p in params],
        out_specs=[x_spec, conv_spec, x_spec],
        out_shape=[jax.ShapeDtypeStruct(x.shape, _F32),
                   jax.ShapeDtypeStruct((nseq, CONV_WIDTH - 1, D_MODEL), _F32),
                   jax.ShapeDtypeStruct(x.shape, _F32)],
        scratch_shapes=_mixer_scratch(nseg, seglen),
        compiler_params=pltpu.CompilerParams(
            dimension_semantics=("arbitrary",), vmem_limit_bytes=VMEM_LIMIT_BYTES),
        name=f"sample_mixer_l{layer}",
    )(x, state_conv, *params)


def _ffn(x, params, norm_final, layer, name):
    nseq, seqlen, _ = x.shape
    if TILE_ROWS <= seqlen:
        grid = (nseq, seqlen // TILE_ROWS)
        x_spec = pl.BlockSpec((None, TILE_ROWS, D_MODEL), lambda b, s: (b, s, 0))
    else:
        per = TILE_ROWS // seqlen
        grid = (nseq // per, 1)
        x_spec = pl.BlockSpec((per, seqlen, D_MODEL), lambda b, s: (b, 0, 0))
    nfin_spec = pl.BlockSpec((1, D_MODEL), lambda b, s: (0, 0))
    return pl.pallas_call(
        functools.partial(_ffn_kernel, sub=FFN_SUB_ROWS, final_norm=layer == DEPTH - 1),
        grid=grid,
        in_specs=[x_spec] + [_resident(p.shape, layer, 2) for p in params] + [nfin_spec],
        out_specs=x_spec,
        out_shape=jax.ShapeDtypeStruct(x.shape, _F32),
        compiler_params=pltpu.CompilerParams(
            dimension_semantics=("arbitrary", "arbitrary"), vmem_limit_bytes=VMEM_LIMIT_BYTES),
        name=f"{name}_ffn_l{layer}",
    )(x, *params, norm_final)


def _matmul_weight(w):
    return jnp.pad(w.astype(_BF16), [(0, 0)] * (w.ndim - 1) + [(0, LANES)])


def _params(norm_mix, w_in, gmlp_ln_g, gmlp_ln_b, w_s, b_s, conv_w, w_pa, w_pb, w_o, norm_ffn,
            w_gate, w_up, w_down, norm_final):
    row = lambda p: p.reshape(DEPTH, 1, D_MODEL)
    bsf = jnp.repeat(jnp.transpose(b_s, (0, 2, 1)), GMLP_HEAD, axis=2)
    mixer_params = (row(norm_mix), _matmul_weight(w_in), row(gmlp_ln_g), row(gmlp_ln_b),
                    w_s.astype(_BF16), bsf, conv_w, _matmul_weight(w_pa), _matmul_weight(w_pb),
                    _matmul_weight(w_o))
    ffn_params = (row(norm_ffn), _matmul_weight(w_gate), _matmul_weight(w_up),
                  _matmul_weight(w_down))
    return mixer_params, ffn_params, norm_final.reshape(1, D_MODEL)


def kernel(x_prompt, x_sample, state_conv, norm_mix, w_in, gmlp_ln_g, gmlp_ln_b, w_s, b_s,
           conv_w, w_pa, w_pb, w_o, norm_ffn, w_gate, w_up, w_down, norm_final):
    assert x_prompt.shape[1] % TILE_ROWS == 0 and TILE_ROWS % SUB_ROWS == 0
    assert SUB_ROWS % GMLP_CHUNK == 0 and SUB_ROWS % x_sample.shape[1] == 0
    assert (x_sample.shape[0] * x_sample.shape[1]) % SUB_ROWS == 0
    assert TILE_ROWS % FFN_SUB_ROWS == 0 and FFN_SUB_ROWS % x_sample.shape[1] == 0
    mixer_params, ffn_params, nfin = _params(
        norm_mix, w_in, gmlp_ln_g, gmlp_ln_b, w_s, b_s, conv_w, w_pa, w_pb, w_o, norm_ffn,
        w_gate, w_up, w_down, norm_final)

    xp, xs = x_prompt, x_sample
    conv_p, conv_s, v_s = [], [], []
    for layer in range(DEPTH):
        xp, cp = _prompt_mixer(xp, mixer_params, layer)
        xp = _ffn(xp, ffn_params, nfin, layer, "prompt")
        xs, cs, vs = _sample_mixer(xs, state_conv, mixer_params, layer)
        xs = _ffn(xs, ffn_params, nfin, layer, "sample")
        conv_p.append(cp)
        conv_s.append(cs)
        v_s.append(vs)
    return (xp, xs, jnp.stack(conv_p), jnp.stack(conv_s), jnp.stack(v_s))
```

```python
import functools

import jax
import jax.numpy as jnp
from jax import lax
from jax.experimental import pallas as pl
from jax.experimental.pallas import tpu as pltpu

D_MODEL = 1024
DEPTH = 4
GMLP_CHUNK = 128
GMLP_GROUPS = 8
GMLP_HEAD = D_MODEL // GMLP_GROUPS
CONV_WIDTH = 3
EPS = 1e-6

SUBLANES = 8
LANES = 128
HALO_PAD = SUBLANES
TILE_ROWS = 1024
SUB_ROWS = 256
FFN_SUB_ROWS = 1024
CAST_ROWS = 256
VMEM_LIMIT_BYTES = 56 * 1024 * 1024

_F32 = jnp.float32
_BF16 = jnp.bfloat16


def _gelu(x):
    return (0.5 * x) * (1.0 + lax.erf(x * (0.5 ** 0.5)))


def _rmsnorm_rows(x, gain):
    r = lax.rsqrt(jnp.mean(x * x, axis=-1, keepdims=True) + EPS)
    return (x * r) * gain


def _mixer_rows(x, w, scratch, set_halo, store_v, *, nseg, seglen, chunk):
    nm_ref, win_ref, lng_ref, lnb_ref, ws_ref, bsf_ref, cw_ref, wpa_ref, wpb_ref, wo_ref = w
    xc_ref, a_ref = scratch
    rows = nseg * seglen
    xn = _rmsnorm_rows(x, nm_ref[...]).astype(_BF16)

    def proj(k):
        return jnp.dot(xn, win_ref[:, k * D_MODEL:(k + 1) * D_MODEL], preferred_element_type=_F32)

    v = _gelu(proj(1))
    mu = jnp.mean(v, axis=-1, keepdims=True)
    vc = v - mu
    var = jnp.mean(vc * vc, axis=-1, keepdims=True)
    v = (vc * lax.rsqrt(var + EPS)) * lng_ref[...] + lnb_ref[...]
    if store_v is not None:
        store_v(v)
    vb = v.astype(_BF16)
    u = _gelu(proj(0))

    ri = lax.broadcasted_iota(jnp.int32, (chunk, chunk), 0)
    ci = lax.broadcasted_iota(jnp.int32, (chunk, chunk), 1)
    tril = ci <= ri
    for g in range(GMLP_GROUPS):
        cols = slice(g * GMLP_HEAD, (g + 1) * GMLP_HEAD)
        wsg = jnp.where(tril, ws_ref[g, :chunk, :chunk], jnp.zeros((), _BF16))
        bias = bsf_ref[:chunk, cols]
        for c in range(rows // chunk):
            rws = slice(c * chunk, (c + 1) * chunk)
            sg = jnp.dot(wsg, vb[rws, cols], preferred_element_type=_F32) + bias
            a_ref[rws, cols] = (u[rws, cols] * sg).astype(_BF16)

    set_halo()
    xc_ref[:, HALO_PAD:HALO_PAD + seglen, :] = (proj(3) * proj(4)).reshape(nseg, seglen, D_MODEL)
    cw = cw_ref[...]
    conv = (cw[0] * xc_ref[:, HALO_PAD - 2:HALO_PAD - 2 + seglen, :]
            + cw[1] * xc_ref[:, HALO_PAD - 1:HALO_PAD - 1 + seglen, :]
            + cw[2] * xc_ref[:, HALO_PAD:HALO_PAD + seglen, :])
    last2 = xc_ref[:, HALO_PAD + seglen - 2:HALO_PAD + seglen, :]
    xc_ref[:, HALO_PAD - 2:HALO_PAD, :] = last2
    bconv = (proj(2) * conv.reshape(rows, D_MODEL)).astype(_BF16)

    ga = jax.nn.sigmoid(proj(5))
    gb = jax.nn.sigmoid(proj(6))
    m = (ga * jnp.dot(a_ref[...], wpa_ref[:, :D_MODEL], preferred_element_type=_F32)
         + gb * jnp.dot(bconv, wpb_ref[:, :D_MODEL], preferred_element_type=_F32))
    return x + jnp.dot(m.astype(_BF16), wo_ref[:, :D_MODEL], preferred_element_type=_F32), last2


def _prompt_mixer_kernel(x_ref, *refs, tile, sub):
    w, (x1_ref, conv_ref), scratch = refs[:10], refs[10:12], refs[12:]
    xc_ref = scratch[0]
    for s in range(tile // sub):
        rws = slice(s * sub, (s + 1) * sub)

        def set_halo(first=(s == 0)):
            if first:
                @pl.when(pl.program_id(1) == 0)
                def _():
                    xc_ref[:, HALO_PAD - 2:HALO_PAD, :] = jnp.zeros((1, 2, D_MODEL), _F32)

        x1, last2 = _mixer_rows(x_ref[rws, :], w, scratch, set_halo, None,
                                nseg=1, seglen=sub, chunk=GMLP_CHUNK)
        x1_ref[rws, :] = x1
        if s == tile // sub - 1:
            conv_ref[...] = last2


def _sample_mixer_kernel(x_ref, halo_ref, *refs, nseq, seglen, sub):
    w, (x1_ref, conv_ref, v_ref), scratch = refs[:10], refs[10:13], refs[13:]
    xc_ref = scratch[0]
    nseg = sub // seglen
    for s in range(nseq // nseg):
        sgs = slice(s * nseg, (s + 1) * nseg)

        def set_halo(sgs=sgs):
            xc_ref[:, HALO_PAD - 2:HALO_PAD, :] = halo_ref[sgs]

        def store_v(v, sgs=sgs):
            v_ref[sgs] = v.reshape(nseg, seglen, D_MODEL)

        x1, last2 = _mixer_rows(x_ref[sgs].reshape(sub, D_MODEL), w, scratch, set_halo,
                                store_v, nseg=nseg, seglen=seglen, chunk=seglen)
        x1_ref[sgs] = x1.reshape(nseg, seglen, D_MODEL)
        conv_ref[sgs] = last2


def _ffn_kernel(x_ref, nf_ref, wg_ref, wu_ref, wd_ref, nfin_ref, y_ref, *, sub, final_norm):
    lead = sub if x_ref.ndim == 2 else sub // x_ref.shape[1]
    for s in range(x_ref.shape[0] // lead):
        sl = slice(s * lead, (s + 1) * lead)
        x = x_ref[sl].reshape(sub, D_MODEL)
        xn = _rmsnorm_rows(x, nf_ref[...]).astype(_BF16)
        hg = jnp.dot(xn, wg_ref[...], preferred_element_type=_F32)
        hu = jnp.dot(xn, wu_ref[...], preferred_element_type=_F32)
        hh = (jax.nn.silu(hg) * hu).astype(_BF16)
        y = x + jnp.dot(hh, wd_ref[...], preferred_element_type=_F32)
        if final_norm:
            y = _rmsnorm_rows(y, nfin_ref[...])
        y_ref[sl] = y.reshape((lead,) + tuple(y_ref.shape[1:]))


def _resident(shape, layer, grid_rank):
    zeros = (0,) * (len(shape) - 1)
    if grid_rank == 1:
        index_map = lambda i: (layer,) + zeros
    else:
        index_map = lambda b, s: (layer,) + zeros
    return pl.BlockSpec((None,) + tuple(shape[1:]), index_map, pipeline_mode=pl.Buffered(1))


def _mixer_scratch(nseg, seglen):
    return [pltpu.VMEM((nseg, HALO_PAD + seglen, D_MODEL), _F32),
            pltpu.VMEM((nseg * seglen, D_MODEL), _BF16)]


def _prompt_mixer(x, params, layer):
    batch, seq, _ = x.shape
    x_spec = pl.BlockSpec((None, TILE_ROWS, D_MODEL), lambda b, s: (b, s, 0))
    return pl.pallas_call(
        functools.partial(_prompt_mixer_kernel, tile=TILE_ROWS, sub=SUB_ROWS),
        grid=(batch, seq // TILE_ROWS),
        in_specs=[x_spec] + [_resident(p.shape, layer, 2) for p in params],
        out_specs=[x_spec, pl.BlockSpec((1, CONV_WIDTH - 1, D_MODEL), lambda b, s: (b, 0, 0))],
        out_shape=[jax.ShapeDtypeStruct(x.shape, _F32),
                   jax.ShapeDtypeStruct((batch, CONV_WIDTH - 1, D_MODEL), _F32)],
        scratch_shapes=_mixer_scratch(1, SUB_ROWS),
        compiler_params=pltpu.CompilerParams(
            dimension_semantics=("arbitrary", "arbitrary"), vmem_limit_bytes=VMEM_LIMIT_BYTES),
        name=f"prompt_mixer_l{layer}",
    )(x, *params)


def _sample_mixer(x, state_conv, params, layer):
    nseq, seglen, _ = x.shape
    nseg = SUB_ROWS // seglen
    whole = lambda i: (0, 0, 0)
    x_spec = pl.BlockSpec((nseq, seglen, D_MODEL), whole)
    conv_spec = pl.BlockSpec((nseq, CONV_WIDTH - 1, D_MODEL), whole)
    halo_spec = pl.BlockSpec((None, nseq, CONV_WIDTH - 1, D_MODEL), lambda i: (layer, 0, 0, 0))
    return pl.pallas_call(
        functools.partial(_sample_mixer_kernel, nseq=nseq, seglen=seglen, sub=SUB_ROWS),
        grid=(1,),
        in_specs=[x_spec, halo_spec] + [_resident(p.shape, layer, 1) for p in params],
        out_specs=[x_spec, conv_spec, x_spec],
        out_shape=[jax.ShapeDtypeStruct(x.shape, _F32),
                   jax.ShapeDtypeStruct((nseq, CONV_WIDTH - 1, D_MODEL), _F32),
                   jax.ShapeDtypeStruct(x.shape, _F32)],
        scratch_shapes=_mixer_scratch(nseg, seglen),
        compiler_params=pltpu.CompilerParams(
            dimension_semantics=("arbitrary",), vmem_limit_bytes=VMEM_LIMIT_BYTES),
        name=f"sample_mixer_l{layer}",
    )(x, state_conv, *params)


def _ffn(x, params, norm_final, layer, name):
    nseq, seqlen, _ = x.shape
    if TILE_ROWS <= seqlen:
        grid = (nseq, seqlen // TILE_ROWS)
        x_spec = pl.BlockSpec((None, TILE_ROWS, D_MODEL), lambda b, s: (b, s, 0))
    else:
        per = TILE_ROWS // seqlen
        grid = (nseq // per, 1)
        x_spec = pl.BlockSpec((per, seqlen, D_MODEL), lambda b, s: (b, 0, 0))
    nfin_spec = pl.BlockSpec((1, D_MODEL), lambda b, s: (0, 0))
    return pl.pallas_call(
        functools.partial(_ffn_kernel, sub=FFN_SUB_ROWS, final_norm=layer == DEPTH - 1),
        grid=grid,
        in_specs=[x_spec] + [_resident(p.shape, layer, 2) for p in params] + [nfin_spec],
        out_specs=x_spec,
        out_shape=jax.ShapeDtypeStruct(x.shape, _F32),
        compiler_params=pltpu.CompilerParams(
            dimension_semantics=("arbitrary", "arbitrary"), vmem_limit_bytes=VMEM_LIMIT_BYTES),
        name=f"{name}_ffn_l{layer}",
    )(x, *params, norm_final)


def _cast_pad_kernel(*refs):
    n = len(refs) // 2
    for w_ref, o_ref in zip(refs[:n], refs[n:]):
        cols = w_ref.shape[-1]
        o_ref[:, :cols] = w_ref[...].astype(_BF16)
        o_ref[:, cols:] = jnp.zeros((o_ref.shape[0], o_ref.shape[1] - cols), _BF16)


def _mixer_weights(*ws):
    depth, rows, _ = ws[0].shape
    assert all(w.shape[:2] == (depth, rows) for w in ws) and rows % CAST_ROWS == 0
    in_spec = lambda w: pl.BlockSpec((None, CAST_ROWS, w.shape[2]), lambda l, r: (l, r, 0))
    out_spec = lambda w: pl.BlockSpec((None, CAST_ROWS, w.shape[2] + LANES), lambda l, r: (l, r, 0))
    return pl.pallas_call(
        _cast_pad_kernel,
        grid=(depth, rows // CAST_ROWS),
        in_specs=[in_spec(w) for w in ws],
        out_specs=[out_spec(w) for w in ws],
        out_shape=[jax.ShapeDtypeStruct((depth, rows, w.shape[2] + LANES), _BF16) for w in ws],
        compiler_params=pltpu.CompilerParams(
            dimension_semantics=("arbitrary", "arbitrary"), vmem_limit_bytes=VMEM_LIMIT_BYTES),
        name="mixer_weight_cast",
    )(*ws)


def _params(norm_mix, w_in, gmlp_ln_g, gmlp_ln_b, w_s, b_s, conv_w, w_pa, w_pb, w_o, norm_ffn,
            w_gate, w_up, w_down, norm_final):
    row = lambda p: p.reshape(DEPTH, 1, D_MODEL)
    bsf = jnp.repeat(jnp.transpose(b_s, (0, 2, 1)), GMLP_HEAD, axis=2)
    w_in_b, w_pa_b, w_pb_b, w_o_b = _mixer_weights(w_in, w_pa, w_pb, w_o)
    mixer_params = (row(norm_mix), w_in_b, row(gmlp_ln_g), row(gmlp_ln_b),
                    w_s.astype(_BF16), bsf, conv_w, w_pa_b, w_pb_b, w_o_b)
    ffn_params = (row(norm_ffn), w_gate.astype(_BF16), w_up.astype(_BF16), w_down.astype(_BF16))
    return mixer_params, ffn_params, norm_final.reshape(1, D_MODEL)


def kernel(x_prompt, x_sample, state_conv, norm_mix, w_in, gmlp_ln_g, gmlp_ln_b, w_s, b_s,
           conv_w, w_pa, w_pb, w_o, norm_ffn, w_gate, w_up, w_down, norm_final):
    assert x_prompt.shape[1] % TILE_ROWS == 0 and TILE_ROWS % SUB_ROWS == 0
    assert SUB_ROWS % GMLP_CHUNK == 0 and SUB_ROWS % x_sample.shape[1] == 0
    assert (x_sample.shape[0] * x_sample.shape[1]) % SUB_ROWS == 0
    assert TILE_ROWS % FFN_SUB_ROWS == 0 and FFN_SUB_ROWS % x_sample.shape[1] == 0
    mixer_params, ffn_params, nfin = _params(
        norm_mix, w_in, gmlp_ln_g, gmlp_ln_b, w_s, b_s, conv_w, w_pa, w_pb, w_o, norm_ffn,
        w_gate, w_up, w_down, norm_final)

    xp, xs = x_prompt, x_sample
    conv_p, conv_s, v_s = [], [], []
    for layer in range(DEPTH):
        xp, cp = _prompt_mixer(xp, mixer_params, layer)
        xp = _ffn(xp, ffn_params, nfin, layer, "prompt")
        xs, cs, vs = _sample_mixer(xs, state_conv, mixer_params, layer)
        xs = _ffn(xs, ffn_params, nfin, layer, "sample")
        conv_p.append(cp)
        conv_s.append(cs)
        v_s.append(vs)
    return (xp, xs, jnp.stack(conv_p), jnp.stack(conv_s), jnp.stack(v_s))
```

```python
import functools
from typing import NamedTuple

import jax
import jax.numpy as jnp
from jax import lax
from jax.experimental import pallas as pl
from jax.experimental.pallas import tpu as pltpu

D_MODEL = 1024
DEPTH = 4
GMLP_CHUNK = 128
GMLP_GROUPS = 8
GMLP_HEAD = D_MODEL // GMLP_GROUPS
CONV_WIDTH = 3
EPS = 1e-6

SUBLANES = 8
LANES = 128
HALO_PAD = SUBLANES
TILE_ROWS = 1024
SUB_ROWS = 256
FFN_SUB_ROWS = 1024
CAST_ROWS = 256
VMEM_LIMIT_BYTES = 56 * 1024 * 1024

_F32 = jnp.float32
_BF16 = jnp.bfloat16


def _gelu(x):
    return (0.5 * x) * (1.0 + lax.erf(x * (0.5 ** 0.5)))


def _rmsnorm_rows(x, gain):
    r = lax.rsqrt(jnp.mean(x * x, axis=-1, keepdims=True) + EPS)
    return (x * r) * gain


def _mixer_rows(x, w, scratch, set_halo, store_v, *, nseg, seglen, chunk):
    nm_ref, win_ref, lng_ref, lnb_ref, ws_ref, bsf_ref, cw_ref, wpa_ref, wpb_ref, wo_ref = w
    xc_ref, a_ref = scratch
    rows = nseg * seglen
    xn = _rmsnorm_rows(x, nm_ref[...]).astype(_BF16)

    def proj(k):
        return jnp.dot(xn, win_ref[:, k * D_MODEL:(k + 1) * D_MODEL], preferred_element_type=_F32)

    v = _gelu(proj(1))
    mu = jnp.mean(v, axis=-1, keepdims=True)
    vc = v - mu
    var = jnp.mean(vc * vc, axis=-1, keepdims=True)
    v = (vc * lax.rsqrt(var + EPS)) * lng_ref[...] + lnb_ref[...]
    if store_v is not None:
        store_v(v)
    vb = v.astype(_BF16)
    u = _gelu(proj(0))

    ri = lax.broadcasted_iota(jnp.int32, (chunk, chunk), 0)
    ci = lax.broadcasted_iota(jnp.int32, (chunk, chunk), 1)
    tril = ci <= ri
    for g in range(GMLP_GROUPS):
        cols = slice(g * GMLP_HEAD, (g + 1) * GMLP_HEAD)
        wsg = jnp.where(tril, ws_ref[g, :chunk, :chunk], jnp.zeros((), _BF16))
        bias = bsf_ref[:chunk, cols]
        for c in range(rows // chunk):
            rws = slice(c * chunk, (c + 1) * chunk)
            sg = jnp.dot(wsg, vb[rws, cols], preferred_element_type=_F32) + bias
            a_ref[rws, cols] = (u[rws, cols] * sg).astype(_BF16)

    set_halo()
    xc_ref[:, HALO_PAD:HALO_PAD + seglen, :] = (proj(3) * proj(4)).reshape(nseg, seglen, D_MODEL)
    cw = cw_ref[...]
    conv = (cw[0] * xc_ref[:, HALO_PAD - 2:HALO_PAD - 2 + seglen, :]
            + cw[1] * xc_ref[:, HALO_PAD - 1:HALO_PAD - 1 + seglen, :]
            + cw[2] * xc_ref[:, HALO_PAD:HALO_PAD + seglen, :])
    last2 = xc_ref[:, HALO_PAD + seglen - 2:HALO_PAD + seglen, :]
    xc_ref[:, HALO_PAD - 2:HALO_PAD, :] = last2
    bconv = (proj(2) * conv.reshape(rows, D_MODEL)).astype(_BF16)

    ga = jax.nn.sigmoid(proj(5))
    gb = jax.nn.sigmoid(proj(6))
    m = (ga * jnp.dot(a_ref[...], wpa_ref[:, :D_MODEL], preferred_element_type=_F32)
         + gb * jnp.dot(bconv, wpb_ref[:, :D_MODEL], preferred_element_type=_F32))
    return x + jnp.dot(m.astype(_BF16), wo_ref[:, :D_MODEL], preferred_element_type=_F32), last2


def _mixer_kernel(xp_ref, xs_ref, halo_ref, *refs, n_prompt, tiles_per_seq, tile, sub):
    w, outs, scratch = refs[:10], refs[10:15], refs[15:]
    x1p_ref, convp_ref, x1s_ref, convs_ref, vs_ref = outs
    xcp_ref, xcs_ref, a_ref = scratch
    step = pl.program_id(0)

    @pl.when(step < n_prompt)
    def _prompt():
        for s in range(tile // sub):
            rws = slice(s * sub, (s + 1) * sub)

            def set_halo(first=(s == 0)):
                if first:
                    @pl.when(lax.rem(step, tiles_per_seq) == 0)
                    def _():
                        xcp_ref[:, HALO_PAD - 2:HALO_PAD, :] = jnp.zeros((1, 2, D_MODEL), _F32)

            x1, last2 = _mixer_rows(xp_ref[rws, :], w, (xcp_ref, a_ref), set_halo, None,
                                    nseg=1, seglen=sub, chunk=GMLP_CHUNK)
            x1p_ref[rws, :] = x1
            if s == tile // sub - 1:
                convp_ref[...] = last2

    @pl.when(step >= n_prompt)
    def _sample():
        nseg, seglen, _ = xs_ref.shape

        def set_halo():
            xcs_ref[:, HALO_PAD - 2:HALO_PAD, :] = halo_ref[...]

        def store_v(v):
            vs_ref[...] = v.reshape(nseg, seglen, D_MODEL)

        x1, last2 = _mixer_rows(xs_ref[...].reshape(sub, D_MODEL), w, (xcs_ref, a_ref), set_halo,
                                store_v, nseg=nseg, seglen=seglen, chunk=seglen)
        x1s_ref[...] = x1.reshape(nseg, seglen, D_MODEL)
        convs_ref[...] = last2


def _ffn_rows(x, nf_ref, wg_ref, wu_ref, wd_ref, nfin_ref, final_norm):
    xn = _rmsnorm_rows(x, nf_ref[...]).astype(_BF16)
    hg = jnp.dot(xn, wg_ref[...], preferred_element_type=_F32)
    hu = jnp.dot(xn, wu_ref[...], preferred_element_type=_F32)
    hh = (jax.nn.silu(hg) * hu).astype(_BF16)
    y = x + jnp.dot(hh, wd_ref[...], preferred_element_type=_F32)
    if final_norm:
        y = _rmsnorm_rows(y, nfin_ref[...])
    return y


def _ffn_kernel(xp_ref, xs_ref, nf_ref, wg_ref, wu_ref, wd_ref, nfin_ref, yp_ref, ys_ref,
                *, n_prompt, sub, final_norm):
    w = (nf_ref, wg_ref, wu_ref, wd_ref, nfin_ref)
    step = pl.program_id(0)

    @pl.when(step < n_prompt)
    def _prompt():
        for s in range(xp_ref.shape[0] // sub):
            rws = slice(s * sub, (s + 1) * sub)
            yp_ref[rws, :] = _ffn_rows(xp_ref[rws, :], *w, final_norm)

    @pl.when(step >= n_prompt)
    def _sample():
        y = _ffn_rows(xs_ref[...].reshape(-1, D_MODEL), *w, final_norm)
        ys_ref[...] = y.reshape(ys_ref.shape)


def _resident(shape, layer):
    zeros = (0,) * (len(shape) - 1)
    return pl.BlockSpec((None,) + tuple(shape[1:]), lambda i: (layer,) + zeros,
                        pipeline_mode=pl.Buffered(1))


class _Steps(NamedTuple):
    n_prompt: int
    n_sample: int
    tiles_per_seq: int
    segs: int

    def prompt_tile(self, i):
        p = jnp.minimum(i, self.n_prompt - 1)
        return (p // self.tiles_per_seq, lax.rem(p, self.tiles_per_seq), 0)

    def prompt_seq(self, i):
        return (jnp.minimum(i, self.n_prompt - 1) // self.tiles_per_seq, 0, 0)

    def sample_index(self, i):
        return jnp.maximum(i - self.n_prompt, 0)

    def sample_block(self, i):
        return (self.sample_index(i), 0, 0)


def _steps(xp, xs):
    batch, seq, _ = xp.shape
    segs = SUB_ROWS // xs.shape[1]
    return _Steps(batch * (seq // TILE_ROWS), xs.shape[0] // segs, seq // TILE_ROWS, segs)


def _mixer(xp, xs, state_conv, params, layer):
    st = _steps(xp, xs)
    batch = xp.shape[0]
    nseq, seglen, _ = xs.shape
    xp_spec = pl.BlockSpec((None, TILE_ROWS, D_MODEL), st.prompt_tile)
    xs_spec = pl.BlockSpec((st.segs, seglen, D_MODEL), st.sample_block)
    convp_spec = pl.BlockSpec((1, CONV_WIDTH - 1, D_MODEL), st.prompt_seq)
    convs_spec = pl.BlockSpec((st.segs, CONV_WIDTH - 1, D_MODEL), st.sample_block)
    halo_spec = pl.BlockSpec((None, st.segs, CONV_WIDTH - 1, D_MODEL),
                             lambda i: (layer, st.sample_index(i), 0, 0))
    return pl.pallas_call(
        functools.partial(_mixer_kernel, n_prompt=st.n_prompt, tiles_per_seq=st.tiles_per_seq,
                          tile=TILE_ROWS, sub=SUB_ROWS),
        grid=(st.n_prompt + st.n_sample,),
        in_specs=[xp_spec, xs_spec, halo_spec] + [_resident(p.shape, layer) for p in params],
        out_specs=[xp_spec, convp_spec, xs_spec, convs_spec, xs_spec],
        out_shape=[jax.ShapeDtypeStruct(xp.shape, _F32),
                   jax.ShapeDtypeStruct((batch, CONV_WIDTH - 1, D_MODEL), _F32),
                   jax.ShapeDtypeStruct(xs.shape, _F32),
                   jax.ShapeDtypeStruct((nseq, CONV_WIDTH - 1, D_MODEL), _F32),
                   jax.ShapeDtypeStruct(xs.shape, _F32)],
        scratch_shapes=[pltpu.VMEM((1, HALO_PAD + SUB_ROWS, D_MODEL), _F32),
                        pltpu.VMEM((st.segs, HALO_PAD + seglen, D_MODEL), _F32),
                        pltpu.VMEM((SUB_ROWS, D_MODEL), _BF16)],
        compiler_params=pltpu.CompilerParams(
            dimension_semantics=("arbitrary",), vmem_limit_bytes=VMEM_LIMIT_BYTES),
        name=f"mixer_l{layer}",
    )(xp, xs, state_conv, *params)


def _ffn(xp, xs, params, norm_final, layer):
    st = _steps(xp, xs)
    xp_spec = pl.BlockSpec((None, TILE_ROWS, D_MODEL), st.prompt_tile)
    xs_spec = pl.BlockSpec((st.segs, xs.shape[1], D_MODEL), st.sample_block)
    nfin_spec = pl.BlockSpec((1, D_MODEL), lambda i: (0, 0))
    return pl.pallas_call(
        functools.partial(_ffn_kernel, n_prompt=st.n_prompt, sub=FFN_SUB_ROWS,
                          final_norm=layer == DEPTH - 1),
        grid=(st.n_prompt + st.n_sample,),
        in_specs=[xp_spec, xs_spec] + [_resident(p.shape, layer) for p in params] + [nfin_spec],
        out_specs=[xp_spec, xs_spec],
        out_shape=[jax.ShapeDtypeStruct(xp.shape, _F32), jax.ShapeDtypeStruct(xs.shape, _F32)],
        compiler_params=pltpu.CompilerParams(
            dimension_semantics=("arbitrary",), vmem_limit_bytes=VMEM_LIMIT_BYTES),
        name=f"ffn_l{layer}",
    )(xp, xs, *params, norm_final)


def _cast_pad_kernel(*refs):
    n = len(refs) // 2
    for w_ref, o_ref in zip(refs[:n], refs[n:]):
        cols = w_ref.shape[-1]
        o_ref[:, :cols] = w_ref[...].astype(_BF16)
        o_ref[:, cols:] = jnp.zeros((o_ref.shape[0], o_ref.shape[1] - cols), _BF16)


def _mixer_weights(*ws):
    depth, rows, _ = ws[0].shape
    assert all(w.shape[:2] == (depth, rows) for w in ws) and rows % CAST_ROWS == 0
    in_spec = lambda w: pl.BlockSpec((None, CAST_ROWS, w.shape[2]), lambda l, r: (l, r, 0))
    out_spec = lambda w: pl.BlockSpec((None, CAST_ROWS, w.shape[2] + LANES), lambda l, r: (l, r, 0))
    return pl.pallas_call(
        _cast_pad_kernel,
        grid=(depth, rows // CAST_ROWS),
        in_specs=[in_spec(w) for w in ws],
        out_specs=[out_spec(w) for w in ws],
        out_shape=[jax.ShapeDtypeStruct((depth, rows, w.shape[2] + LANES), _BF16) for w in ws],
        compiler_params=pltpu.CompilerParams(
            dimension_semantics=("arbitrary", "arbitrary"), vmem_limit_bytes=VMEM_LIMIT_BYTES),
        name="mixer_weight_cast",
    )(*ws)


def _params(norm_mix, w_in, gmlp_ln_g, gmlp_ln_b, w_s, b_s, conv_w, w_pa, w_pb, w_o, norm_ffn,
            w_gate, w_up, w_down, norm_final):
    row = lambda p: p.reshape(DEPTH, 1, D_MODEL)
    bsf = jnp.repeat(jnp.transpose(b_s, (0, 2, 1)), GMLP_HEAD, axis=2)
    w_in_b, w_pa_b, w_pb_b, w_o_b = _mixer_weights(w_in, w_pa, w_pb, w_o)
    mixer_params = (row(norm_mix), w_in_b, row(gmlp_ln_g), row(gmlp_ln_b),
                    w_s.astype(_BF16), bsf, conv_w, w_pa_b, w_pb_b, w_o_b)
    ffn_params = (row(norm_ffn), w_gate.astype(_BF16), w_up.astype(_BF16), w_down.astype(_BF16))
    return mixer_params, ffn_params, norm_final.reshape(1, D_MODEL)


def kernel(x_prompt, x_sample, state_conv, norm_mix, w_in, gmlp_ln_g, gmlp_ln_b, w_s, b_s,
           conv_w, w_pa, w_pb, w_o, norm_ffn, w_gate, w_up, w_down, norm_final):
    assert x_prompt.shape[1] % TILE_ROWS == 0 and TILE_ROWS % SUB_ROWS == 0
    assert SUB_ROWS % GMLP_CHUNK == 0 and SUB_ROWS % x_sample.shape[1] == 0
    assert (x_sample.shape[0] * x_sample.shape[1]) % SUB_ROWS == 0
    assert TILE_ROWS % FFN_SUB_ROWS == 0
    mixer_params, ffn_params, nfin = _params(
        norm_mix, w_in, gmlp_ln_g, gmlp_ln_b, w_s, b_s, conv_w, w_pa, w_pb, w_o, norm_ffn,
        w_gate, w_up, w_down, norm_final)

    xp, xs = x_prompt, x_sample
    conv_p, conv_s, v_s = [], [], []
    for layer in range(DEPTH):
        xp, cp, xs, cs, vs = _mixer(xp, xs, state_conv, mixer_params, layer)
        xp, xs = _ffn(xp, xs, ffn_params, nfin, layer)
        conv_p.append(cp)
        conv_s.append(cs)
        v_s.append(vs)
    return (xp, xs, jnp.stack(conv_p), jnp.stack(conv_s), jnp.stack(v_s))
```

```python
import functools

import jax
import jax.numpy as jnp
from jax import lax
from jax.experimental import pallas as pl
from jax.experimental.pallas import tpu as pltpu

D_MODEL = 1024
DEPTH = 4
GMLP_CHUNK = 128
GMLP_GROUPS = 8
GMLP_HEAD = D_MODEL // GMLP_GROUPS
CONV_WIDTH = 3
EPS = 1e-6

SUBLANES = 8
LANES = 128
HALO_PAD = SUBLANES
MIXER_TILE_ROWS = 2048
TILE_ROWS = 1024
SUB_ROWS = 256
FFN_SUB_ROWS = 1024
CAST_ROWS = 256
VMEM_LIMIT_BYTES = 56 * 1024 * 1024
MIXER_VMEM_LIMIT_BYTES = 62 * 1024 * 1024

_F32 = jnp.float32
_BF16 = jnp.bfloat16


def _gelu(x):
    return (0.5 * x) * (1.0 + lax.erf(x * (0.5 ** 0.5)))


def _rmsnorm_rows(x, gain):
    r = lax.rsqrt(jnp.mean(x * x, axis=-1, keepdims=True) + EPS)
    return (x * r) * gain


def _mixer_rows(x, w, scratch, set_halo, store_v, *, nseg, seglen, chunk):
    nm_ref, win_ref, lng_ref, lnb_ref, ws_ref, bsf_ref, cw_ref, wpa_ref, wpb_ref, wo_ref = w
    xc_ref, a_ref = scratch
    rows = nseg * seglen
    xn = _rmsnorm_rows(x, nm_ref[...]).astype(_BF16)

    def proj(k):
        return jnp.dot(xn, win_ref[:, k * D_MODEL:(k + 1) * D_MODEL], preferred_element_type=_F32)

    v = _gelu(proj(1))
    mu = jnp.mean(v, axis=-1, keepdims=True)
    vc = v - mu
    var = jnp.mean(vc * vc, axis=-1, keepdims=True)
    v = (vc * lax.rsqrt(var + EPS)) * lng_ref[...] + lnb_ref[...]
    if store_v is not None:
        store_v(v)
    vb = v.astype(_BF16)
    u = _gelu(proj(0))

    ri = lax.broadcasted_iota(jnp.int32, (chunk, chunk), 0)
    ci = lax.broadcasted_iota(jnp.int32, (chunk, chunk), 1)
    tril = ci <= ri
    for g in range(GMLP_GROUPS):
        cols = slice(g * GMLP_HEAD, (g + 1) * GMLP_HEAD)
        wsg = jnp.where(tril, ws_ref[g, :chunk, :chunk], jnp.zeros((), _BF16))
        bias = bsf_ref[:chunk, cols]
        for c in range(rows // chunk):
            rws = slice(c * chunk, (c + 1) * chunk)
            sg = jnp.dot(wsg, vb[rws, cols], preferred_element_type=_F32) + bias
            a_ref[rws, cols] = (u[rws, cols] * sg).astype(_BF16)

    set_halo()
    xc_ref[:, HALO_PAD:HALO_PAD + seglen, :] = (proj(3) * proj(4)).reshape(nseg, seglen, D_MODEL)
    cw = cw_ref[...]
    conv = (cw[0] * xc_ref[:, HALO_PAD - 2:HALO_PAD - 2 + seglen, :]
            + cw[1] * xc_ref[:, HALO_PAD - 1:HALO_PAD - 1 + seglen, :]
            + cw[2] * xc_ref[:, HALO_PAD:HALO_PAD + seglen, :])
    last2 = xc_ref[:, HALO_PAD + seglen - 2:HALO_PAD + seglen, :]
    xc_ref[:, HALO_PAD - 2:HALO_PAD, :] = last2
    bconv = (proj(2) * conv.reshape(rows, D_MODEL)).astype(_BF16)

    ga = jax.nn.sigmoid(proj(5))
    gb = jax.nn.sigmoid(proj(6))
    m = (ga * jnp.dot(a_ref[...], wpa_ref[:, :D_MODEL], preferred_element_type=_F32)
         + gb * jnp.dot(bconv, wpb_ref[:, :D_MODEL], preferred_element_type=_F32))
    return x + jnp.dot(m.astype(_BF16), wo_ref[:, :D_MODEL], preferred_element_type=_F32), last2


def _prompt_mixer_kernel(x_ref, *refs, tile, sub):
    w, (x1_ref, conv_ref), scratch = refs[:10], refs[10:12], refs[12:]
    xc_ref = scratch[0]
    for s in range(tile // sub):
        rws = slice(s * sub, (s + 1) * sub)

        def set_halo(first=(s == 0)):
            if first:
                @pl.when(pl.program_id(1) == 0)
                def _():
                    xc_ref[:, HALO_PAD - 2:HALO_PAD, :] = jnp.zeros((1, 2, D_MODEL), _F32)

        x1, last2 = _mixer_rows(x_ref[rws, :], w, scratch, set_halo, None,
                                nseg=1, seglen=sub, chunk=GMLP_CHUNK)
        x1_ref[rws, :] = x1
        if s == tile // sub - 1:
            conv_ref[...] = last2


def _sample_mixer_kernel(x_ref, halo_ref, *refs, nseq, seglen, sub):
    w, (x1_ref, conv_ref, v_ref), scratch = refs[:10], refs[10:13], refs[13:]
    xc_ref = scratch[0]
    nseg = sub // seglen
    for s in range(nseq // nseg):
        sgs = slice(s * nseg, (s + 1) * nseg)

        def set_halo(sgs=sgs):
            xc_ref[:, HALO_PAD - 2:HALO_PAD, :] = halo_ref[sgs]

        def store_v(v, sgs=sgs):
            v_ref[sgs] = v.reshape(nseg, seglen, D_MODEL)

        x1, last2 = _mixer_rows(x_ref[sgs].reshape(sub, D_MODEL), w, scratch, set_halo,
                                store_v, nseg=nseg, seglen=seglen, chunk=seglen)
        x1_ref[sgs] = x1.reshape(nseg, seglen, D_MODEL)
        conv_ref[sgs] = last2


def _ffn_kernel(x_ref, nf_ref, wg_ref, wu_ref, wd_ref, nfin_ref, y_ref, *, sub, final_norm):
    lead = sub if x_ref.ndim == 2 else sub // x_ref.shape[1]
    for s in range(x_ref.shape[0] // lead):
        sl = slice(s * lead, (s + 1) * lead)
        x = x_ref[sl].reshape(sub, D_MODEL)
        xn = _rmsnorm_rows(x, nf_ref[...]).astype(_BF16)
        hg = jnp.dot(xn, wg_ref[...], preferred_element_type=_F32)
        hu = jnp.dot(xn, wu_ref[...], preferred_element_type=_F32)
        hh = (jax.nn.silu(hg) * hu).astype(_BF16)
        y = x + jnp.dot(hh, wd_ref[...], preferred_element_type=_F32)
        if final_norm:
            y = _rmsnorm_rows(y, nfin_ref[...])
        y_ref[sl] = y.reshape((lead,) + tuple(y_ref.shape[1:]))


def _resident(shape, layer, grid_rank):
    zeros = (0,) * (len(shape) - 1)
    if grid_rank == 1:
        index_map = lambda i: (layer,) + zeros
    else:
        index_map = lambda b, s: (layer,) + zeros
    return pl.BlockSpec((None,) + tuple(shape[1:]), index_map, pipeline_mode=pl.Buffered(1))


def _mixer_scratch(nseg, seglen):
    return [pltpu.VMEM((nseg, HALO_PAD + seglen, D_MODEL), _F32),
            pltpu.VMEM((nseg * seglen, D_MODEL), _BF16)]


def _prompt_mixer(x, params, layer):
    batch, seq, _ = x.shape
    x_spec = pl.BlockSpec((None, MIXER_TILE_ROWS, D_MODEL), lambda b, s: (b, s, 0))
    return pl.pallas_call(
        functools.partial(_prompt_mixer_kernel, tile=MIXER_TILE_ROWS, sub=SUB_ROWS),
        grid=(batch, seq // MIXER_TILE_ROWS),
        in_specs=[x_spec] + [_resident(p.shape, layer, 2) for p in params],
        out_specs=[x_spec, pl.BlockSpec((1, CONV_WIDTH - 1, D_MODEL), lambda b, s: (b, 0, 0))],
        out_shape=[jax.ShapeDtypeStruct(x.shape, _F32),
                   jax.ShapeDtypeStruct((batch, CONV_WIDTH - 1, D_MODEL), _F32)],
        scratch_shapes=_mixer_scratch(1, SUB_ROWS),
        compiler_params=pltpu.CompilerParams(
            dimension_semantics=("arbitrary", "arbitrary"),
            vmem_limit_bytes=MIXER_VMEM_LIMIT_BYTES),
        name=f"prompt_mixer_l{layer}",
    )(x, *params)


def _sample_mixer(x, state_conv, params, layer):
    nseq, seglen, _ = x.shape
    nseg = SUB_ROWS // seglen
    whole = lambda i: (0, 0, 0)
    x_spec = pl.BlockSpec((nseq, seglen, D_MODEL), whole)
    conv_spec = pl.BlockSpec((nseq, CONV_WIDTH - 1, D_MODEL), whole)
    halo_spec = pl.BlockSpec((None, nseq, CONV_WIDTH - 1, D_MODEL), lambda i: (layer, 0, 0, 0))
    return pl.pallas_call(
        functools.partial(_sample_mixer_kernel, nseq=nseq, seglen=seglen, sub=SUB_ROWS),
        grid=(1,),
        in_specs=[x_spec, halo_spec] + [_resident(p.shape, layer, 1) for p in params],
        out_specs=[x_spec, conv_spec, x_spec],
        out_shape=[jax.ShapeDtypeStruct(x.shape, _F32),
                   jax.ShapeDtypeStruct((nseq, CONV_WIDTH - 1, D_MODEL), _F32),
                   jax.ShapeDtypeStruct(x.shape, _F32)],
        scratch_shapes=_mixer_scratch(nseg, seglen),
        compiler_params=pltpu.CompilerParams(
            dimension_semantics=("arbitrary",), vmem_limit_bytes=VMEM_LIMIT_BYTES),
        name=f"sample_mixer_l{layer}",
    )(x, state_conv, *params)


def _ffn(x, params, norm_final, layer, name):
    nseq, seqlen, _ = x.shape
    if TILE_ROWS <= seqlen:
        grid = (nseq, seqlen // TILE_ROWS)
        x_spec = pl.BlockSpec((None, TILE_ROWS, D_MODEL), lambda b, s: (b, s, 0))
    else:
        per = TILE_ROWS // seqlen
        grid = (nseq // per, 1)
        x_spec = pl.BlockSpec((per, seqlen, D_MODEL), lambda b, s: (b, 0, 0))
    nfin_spec = pl.BlockSpec((1, D_MODEL), lambda b, s: (0, 0))
    return pl.pallas_call(
        functools.partial(_ffn_kernel, sub=FFN_SUB_ROWS, final_norm=layer == DEPTH - 1),
        grid=grid,
        in_specs=[x_spec] + [_resident(p.shape, layer, 2) for p in params] + [nfin_spec],
        out_specs=x_spec,
        out_shape=jax.ShapeDtypeStruct(x.shape, _F32),
        compiler_params=pltpu.CompilerParams(
            dimension_semantics=("arbitrary", "arbitrary"), vmem_limit_bytes=VMEM_LIMIT_BYTES),
        name=f"{name}_ffn_l{layer}",
    )(x, *params, norm_final)


def _cast_pad_kernel(*refs):
    n = len(refs) // 2
    for w_ref, o_ref in zip(refs[:n], refs[n:]):
        cols = w_ref.shape[-1]
        o_ref[:, :cols] = w_ref[...].astype(_BF16)
        o_ref[:, cols:] = jnp.zeros((o_ref.shape[0], o_ref.shape[1] - cols), _BF16)


def _mixer_weights(*ws):
    depth, rows, _ = ws[0].shape
    assert all(w.shape[:2] == (depth, rows) for w in ws) and rows % CAST_ROWS == 0
    in_spec = lambda w: pl.BlockSpec((None, CAST_ROWS, w.shape[2]), lambda l, r: (l, r, 0))
    out_spec = lambda w: pl.BlockSpec((None, CAST_ROWS, w.shape[2] + LANES), lambda l, r: (l, r, 0))
    return pl.pallas_call(
        _cast_pad_kernel,
        grid=(depth, rows // CAST_ROWS),
        in_specs=[in_spec(w) for w in ws],
        out_specs=[out_spec(w) for w in ws],
        out_shape=[jax.ShapeDtypeStruct((depth, rows, w.shape[2] + LANES), _BF16) for w in ws],
        compiler_params=pltpu.CompilerParams(
            dimension_semantics=("arbitrary", "arbitrary"), vmem_limit_bytes=VMEM_LIMIT_BYTES),
        name="mixer_weight_cast",
    )(*ws)


def _params(norm_mix, w_in, gmlp_ln_g, gmlp_ln_b, w_s, b_s, conv_w, w_pa, w_pb, w_o, norm_ffn,
            w_gate, w_up, w_down, norm_final):
    row = lambda p: p.reshape(DEPTH, 1, D_MODEL)
    bsf = jnp.repeat(jnp.transpose(b_s, (0, 2, 1)), GMLP_HEAD, axis=2)
    w_in_b, w_pa_b, w_pb_b, w_o_b = _mixer_weights(w_in, w_pa, w_pb, w_o)
    mixer_params = (row(norm_mix), w_in_b, row(gmlp_ln_g), row(gmlp_ln_b),
                    w_s.astype(_BF16), bsf, conv_w, w_pa_b, w_pb_b, w_o_b)
    ffn_params = (row(norm_ffn), w_gate.astype(_BF16), w_up.astype(_BF16), w_down.astype(_BF16))
    return mixer_params, ffn_params, norm_final.reshape(1, D_MODEL)


def kernel(x_prompt, x_sample, state_conv, norm_mix, w_in, gmlp_ln_g, gmlp_ln_b, w_s, b_s,
           conv_w, w_pa, w_pb, w_o, norm_ffn, w_gate, w_up, w_down, norm_final):
    assert x_prompt.shape[1] % MIXER_TILE_ROWS == 0 and MIXER_TILE_ROWS % SUB_ROWS == 0
    assert x_prompt.shape[1] % TILE_ROWS == 0
    assert SUB_ROWS % GMLP_CHUNK == 0 and SUB_ROWS % x_sample.shape[1] == 0
    assert (x_sample.shape[0] * x_sample.shape[1]) % SUB_ROWS == 0
    assert TILE_ROWS % FFN_SUB_ROWS == 0 and FFN_SUB_ROWS % x_sample.shape[1] == 0
    mixer_params, ffn_params, nfin = _params(
        norm_mix, w_in, gmlp_ln_g, gmlp_ln_b, w_s, b_s, conv_w, w_pa, w_pb, w_o, norm_ffn,
        w_gate, w_up, w_down, norm_final)

    xp, xs = x_prompt, x_sample
    conv_p, conv_s, v_s = [], [], []
    for layer in range(DEPTH):
        xp, cp = _prompt_mixer(xp, mixer_params, layer)
        xp = _ffn(xp, ffn_params, nfin, layer, "prompt")
        xs, cs, vs = _sample_mixer(xs, state_conv, mixer_params, layer)
        xs = _ffn(xs, ffn_params, nfin, layer, "sample")
        conv_p.append(cp)
        conv_s.append(cs)
        v_s.append(vs)
    return (xp, xs, jnp.stack(conv_p), jnp.stack(conv_s), jnp.stack(v_s))
```

```python
import functools

import jax
import jax.numpy as jnp
from jax import lax
from jax.experimental import pallas as pl
from jax.experimental.pallas import tpu as pltpu

D_MODEL = 1024
DEPTH = 4
GMLP_CHUNK = 128
GMLP_GROUPS = 8
GMLP_HEAD = D_MODEL // GMLP_GROUPS
CONV_WIDTH = 3
EPS = 1e-6

SUBLANES = 8
LANES = 128
HALO_PAD = SUBLANES
TILE_ROWS = 1024
SUB_ROWS = 256
FFN_SUB_ROWS = 256
CAST_ROWS = 256
VMEM_LIMIT_BYTES = 56 * 1024 * 1024

_F32 = jnp.float32
_BF16 = jnp.bfloat16


def _gelu(x):
    return (0.5 * x) * (1.0 + lax.erf(x * (0.5 ** 0.5)))


def _rmsnorm_rows(x, gain):
    r = lax.rsqrt(jnp.mean(x * x, axis=-1, keepdims=True) + EPS)
    return (x * r) * gain


def _mixer_rows(x, w, scratch, set_halo, store_v, *, nseg, seglen, chunk):
    nm_ref, win_ref, lng_ref, lnb_ref, ws_ref, bsf_ref, cw_ref, wpa_ref, wpb_ref, wo_ref = w
    xc_ref, a_ref = scratch
    rows = nseg * seglen
    xn = _rmsnorm_rows(x, nm_ref[...]).astype(_BF16)

    def proj(k):
        return jnp.dot(xn, win_ref[:, k * D_MODEL:(k + 1) * D_MODEL], preferred_element_type=_F32)

    v = _gelu(proj(1))
    mu = jnp.mean(v, axis=-1, keepdims=True)
    vc = v - mu
    var = jnp.mean(vc * vc, axis=-1, keepdims=True)
    v = (vc * lax.rsqrt(var + EPS)) * lng_ref[...] + lnb_ref[...]
    if store_v is not None:
        store_v(v)
    vb = v.astype(_BF16)
    u = _gelu(proj(0))

    ri = lax.broadcasted_iota(jnp.int32, (chunk, chunk), 0)
    ci = lax.broadcasted_iota(jnp.int32, (chunk, chunk), 1)
    tril = ci <= ri
    for g in range(GMLP_GROUPS):
        cols = slice(g * GMLP_HEAD, (g + 1) * GMLP_HEAD)
        wsg = jnp.where(tril, ws_ref[g, :chunk, :chunk], jnp.zeros((), _BF16))
        bias = bsf_ref[:chunk, cols]
        for c in range(rows // chunk):
            rws = slice(c * chunk, (c + 1) * chunk)
            sg = jnp.dot(wsg, vb[rws, cols], preferred_element_type=_F32) + bias
            a_ref[rws, cols] = (u[rws, cols] * sg).astype(_BF16)

    set_halo()
    xc_ref[:, HALO_PAD:HALO_PAD + seglen, :] = (proj(3) * proj(4)).reshape(nseg, seglen, D_MODEL)
    cw = cw_ref[...]
    conv = (cw[0] * xc_ref[:, HALO_PAD - 2:HALO_PAD - 2 + seglen, :]
            + cw[1] * xc_ref[:, HALO_PAD - 1:HALO_PAD - 1 + seglen, :]
            + cw[2] * xc_ref[:, HALO_PAD:HALO_PAD + seglen, :])
    last2 = xc_ref[:, HALO_PAD + seglen - 2:HALO_PAD + seglen, :]
    xc_ref[:, HALO_PAD - 2:HALO_PAD, :] = last2
    bconv = (proj(2) * conv.reshape(rows, D_MODEL)).astype(_BF16)

    ga = jax.nn.sigmoid(proj(5))
    gb = jax.nn.sigmoid(proj(6))
    m = (ga * jnp.dot(a_ref[...], wpa_ref[:, :D_MODEL], preferred_element_type=_F32)
         + gb * jnp.dot(bconv, wpb_ref[:, :D_MODEL], preferred_element_type=_F32))
    return x + jnp.dot(m.astype(_BF16), wo_ref[:, :D_MODEL], preferred_element_type=_F32), last2


def _prompt_mixer_kernel(x_ref, *refs, tile, sub):
    w, (x1_ref, conv_ref), scratch = refs[:10], refs[10:12], refs[12:]
    xc_ref = scratch[0]
    for s in range(tile // sub):
        rws = slice(s * sub, (s + 1) * sub)

        def set_halo(first=(s == 0)):
            if first:
                @pl.when(pl.program_id(1) == 0)
                def _():
                    xc_ref[:, HALO_PAD - 2:HALO_PAD, :] = jnp.zeros((1, 2, D_MODEL), _F32)

        x1, last2 = _mixer_rows(x_ref[rws, :], w, scratch, set_halo, None,
                                nseg=1, seglen=sub, chunk=GMLP_CHUNK)
        x1_ref[rws, :] = x1
        if s == tile // sub - 1:
            conv_ref[...] = last2


def _sample_mixer_kernel(x_ref, halo_ref, *refs, nseq, seglen, sub):
    w, (x1_ref, conv_ref, v_ref), scratch = refs[:10], refs[10:13], refs[13:]
    xc_ref = scratch[0]
    nseg = sub // seglen
    for s in range(nseq // nseg):
        sgs = slice(s * nseg, (s + 1) * nseg)

        def set_halo(sgs=sgs):
            xc_ref[:, HALO_PAD - 2:HALO_PAD, :] = halo_ref[sgs]

        def store_v(v, sgs=sgs):
            v_ref[sgs] = v.reshape(nseg, seglen, D_MODEL)

        x1, last2 = _mixer_rows(x_ref[sgs].reshape(sub, D_MODEL), w, scratch, set_halo,
                                store_v, nseg=nseg, seglen=seglen, chunk=seglen)
        x1_ref[sgs] = x1.reshape(nseg, seglen, D_MODEL)
        conv_ref[sgs] = last2


def _ffn_kernel(x_ref, nf_ref, wg_ref, wu_ref, wd_ref, nfin_ref, y_ref, *, sub, final_norm):
    lead = sub if x_ref.ndim == 2 else sub // x_ref.shape[1]
    for s in range(x_ref.shape[0] // lead):
        sl = slice(s * lead, (s + 1) * lead)
        x = x_ref[sl].reshape(sub, D_MODEL)
        xn = _rmsnorm_rows(x, nf_ref[...]).astype(_BF16)
        hg = jnp.dot(xn, wg_ref[...], preferred_element_type=_F32)
        hu = jnp.dot(xn, wu_ref[...], preferred_element_type=_F32)
        hh = (jax.nn.silu(hg) * hu).astype(_BF16)
        y = x + jnp.dot(hh, wd_ref[...], preferred_element_type=_F32)
        if final_norm:
            y = _rmsnorm_rows(y, nfin_ref[...])
        y_ref[sl] = y.reshape((lead,) + tuple(y_ref.shape[1:]))


def _resident(shape, layer, grid_rank):
    zeros = (0,) * (len(shape) - 1)
    if grid_rank == 1:
        index_map = lambda i: (layer,) + zeros
    else:
        index_map = lambda b, s: (layer,) + zeros
    return pl.BlockSpec((None,) + tuple(shape[1:]), index_map, pipeline_mode=pl.Buffered(1))


def _mixer_scratch(nseg, seglen):
    return [pltpu.VMEM((nseg, HALO_PAD + seglen, D_MODEL), _F32),
            pltpu.VMEM((nseg * seglen, D_MODEL), _BF16)]


def _prompt_mixer(x, params, layer):
    batch, seq, _ = x.shape
    x_spec = pl.BlockSpec((None, TILE_ROWS, D_MODEL), lambda b, s: (b, s, 0))
    return pl.pallas_call(
        functools.partial(_prompt_mixer_kernel, tile=TILE_ROWS, sub=SUB_ROWS),
        grid=(batch, seq // TILE_ROWS),
        in_specs=[x_spec] + [_resident(p.shape, layer, 2) for p in params],
        out_specs=[x_spec, pl.BlockSpec((1, CONV_WIDTH - 1, D_MODEL), lambda b, s: (b, 0, 0))],
        out_shape=[jax.ShapeDtypeStruct(x.shape, _F32),
                   jax.ShapeDtypeStruct((batch, CONV_WIDTH - 1, D_MODEL), _F32)],
        scratch_shapes=_mixer_scratch(1, SUB_ROWS),
        compiler_params=pltpu.CompilerParams(
            dimension_semantics=("arbitrary", "arbitrary"), vmem_limit_bytes=VMEM_LIMIT_BYTES),
        name=f"prompt_mixer_l{layer}",
    )(x, *params)


def _sample_mixer(x, state_conv, params, layer):
    nseq, seglen, _ = x.shape
    nseg = SUB_ROWS // seglen
    whole = lambda i: (0, 0, 0)
    x_spec = pl.BlockSpec((nseq, seglen, D_MODEL), whole)
    conv_spec = pl.BlockSpec((nseq, CONV_WIDTH - 1, D_MODEL), whole)
    halo_spec = pl.BlockSpec((None, nseq, CONV_WIDTH - 1, D_MODEL), lambda i: (layer, 0, 0, 0))
    return pl.pallas_call(
        functools.partial(_sample_mixer_kernel, nseq=nseq, seglen=seglen, sub=SUB_ROWS),
        grid=(1,),
        in_specs=[x_spec, halo_spec] + [_resident(p.shape, layer, 1) for p in params],
        out_specs=[x_spec, conv_spec, x_spec],
        out_shape=[jax.ShapeDtypeStruct(x.shape, _F32),
                   jax.ShapeDtypeStruct((nseq, CONV_WIDTH - 1, D_MODEL), _F32),
                   jax.ShapeDtypeStruct(x.shape, _F32)],
        scratch_shapes=_mixer_scratch(nseg, seglen),
        compiler_params=pltpu.CompilerParams(
            dimension_semantics=("arbitrary",), vmem_limit_bytes=VMEM_LIMIT_BYTES),
        name=f"sample_mixer_l{layer}",
    )(x, state_conv, *params)


def _ffn(x, params, norm_final, layer, name):
    nseq, seqlen, _ = x.shape
    if TILE_ROWS <= seqlen:
        grid = (nseq, seqlen // TILE_ROWS)
        x_spec = pl.BlockSpec((None, TILE_ROWS, D_MODEL), lambda b, s: (b, s, 0))
    else:
        per = TILE_ROWS // seqlen
        grid = (nseq // per, 1)
        x_spec = pl.BlockSpec((per, seqlen, D_MODEL), lambda b, s: (b, 0, 0))
    nfin_spec = pl.BlockSpec((1, D_MODEL), lambda b, s: (0, 0))
    return pl.pallas_call(
        functools.partial(_ffn_kernel, sub=FFN_SUB_ROWS, final_norm=layer == DEPTH - 1),
        grid=grid,
        in_specs=[x_spec] + [_resident(p.shape, layer, 2) for p in params] + [nfin_spec],
        out_specs=x_spec,
        out_shape=jax.ShapeDtypeStruct(x.shape, _F32),
        compiler_params=pltpu.CompilerParams(
            dimension_semantics=("arbitrary", "arbitrary"), vmem_limit_bytes=VMEM_LIMIT_BYTES),
        name=f"{name}_ffn_l{layer}",
    )(x, *params, norm_final)


def _cast_pad_kernel(*refs):
    n = len(refs) // 2
    for w_ref, o_ref in zip(refs[:n], refs[n:]):
        cols = w_ref.shape[-1]
        o_ref[:, :cols] = w_ref[...].astype(_BF16)
        o_ref[:, cols:] = jnp.zeros((o_ref.shape[0], o_ref.shape[1] - cols), _BF16)


def _mixer_weights(*ws):
    depth, rows, _ = ws[0].shape
    assert all(w.shape[:2] == (depth, rows) for w in ws) and rows % CAST_ROWS == 0
    in_spec = lambda w: pl.BlockSpec((None, CAST_ROWS, w.shape[2]), lambda l, r: (l, r, 0))
    out_spec = lambda w: pl.BlockSpec((None, CAST_ROWS, w.shape[2] + LANES), lambda l, r: (l, r, 0))
    return pl.pallas_call(
        _cast_pad_kernel,
        grid=(depth, rows // CAST_ROWS),
        in_specs=[in_spec(w) for w in ws],
        out_specs=[out_spec(w) for w in ws],
        out_shape=[jax.ShapeDtypeStruct((depth, rows, w.shape[2] + LANES), _BF16) for w in ws],
        compiler_params=pltpu.CompilerParams(
            dimension_semantics=("arbitrary", "arbitrary"), vmem_limit_bytes=VMEM_LIMIT_BYTES),
        name="mixer_weight_cast",
    )(*ws)


def _params(norm_mix, w_in, gmlp_ln_g, gmlp_ln_b, w_s, b_s, conv_w, w_pa, w_pb, w_o, norm_ffn,
            w_gate, w_up, w_down, norm_final):
    row = lambda p: p.reshape(DEPTH, 1, D_MODEL)
    bsf = jnp.repeat(jnp.transpose(b_s, (0, 2, 1)), GMLP_HEAD, axis=2)
    w_in_b, w_pa_b, w_pb_b, w_o_b = _mixer_weights(w_in, w_pa, w_pb, w_o)
    mixer_params = (row(norm_mix), w_in_b, row(gmlp_ln_g), row(gmlp_ln_b),
                    w_s.astype(_BF16), bsf, conv_w, w_pa_b, w_pb_b, w_o_b)
    ffn_params = (row(norm_ffn), w_gate.astype(_BF16), w_up.astype(_BF16), w_down.astype(_BF16))
    return mixer_params, ffn_params, norm_final.reshape(1, D_MODEL)


def kernel(x_prompt, x_sample, state_conv, norm_mix, w_in, gmlp_ln_g, gmlp_ln_b, w_s, b_s,
           conv_w, w_pa, w_pb, w_o, norm_ffn, w_gate, w_up, w_down, norm_final):
    assert x_prompt.shape[1] % TILE_ROWS == 0 and TILE_ROWS % SUB_ROWS == 0
    assert SUB_ROWS % GMLP_CHUNK == 0 and SUB_ROWS % x_sample.shape[1] == 0
    assert (x_sample.shape[0] * x_sample.shape[1]) % SUB_ROWS == 0
    assert TILE_ROWS % FFN_SUB_ROWS == 0 and FFN_SUB_ROWS % x_sample.shape[1] == 0
    mixer_params, ffn_params, nfin = _params(
        norm_mix, w_in, gmlp_ln_g, gmlp_ln_b, w_s, b_s, conv_w, w_pa, w_pb, w_o, norm_ffn,
        w_gate, w_up, w_down, norm_final)

    xp, xs = x_prompt, x_sample
    conv_p, conv_s, v_s = [], [], []
    for layer in range(DEPTH):
        xp, cp = _prompt_mixer(xp, mixer_params, layer)
        xp = _ffn(xp, ffn_params, nfin, layer, "prompt")
        xs, cs, vs = _sample_mixer(xs, state_conv, mixer_params, layer)
        xs = _ffn(xs, ffn_params, nfin, layer, "sample")
        conv_p.append(cp)
        conv_s.append(cs)
        v_s.append(vs)
    return (xp, xs, jnp.stack(conv_p), jnp.stack(conv_s), jnp.stack(v_s))
```
